```python
import math
import jax, jax.numpy as jnp
from jax import lax
import numpy as np

D_MODEL = 1024
BATCH = 4
SEQ = 4096
DEPTH = 1
DEC_BATCH = 32
DEC_SEQ = 64
PAST_LEN = 4096

CHUNK = 64
Q_BLOCK = 128
RMS_EPS = 1e-6
SUBLN_EPS = 1e-5
ROPE_THETA = 10000.0
DA_HEADS = 8
DA_DK = 64
DA_DV = 2 * DA_DK
DA_QK_WIDTH = DA_HEADS * 2 * DA_DK
DA_V_WIDTH = DA_HEADS * DA_DV
RG_WIDTH = D_MODEL
RG_BLOCKS = 16
RG_BW = RG_WIDTH // RG_BLOCKS
RG_C = 8.0
CONV_W = 4
N_BRANCH = 2
IN_COLS = 2 * DA_QK_WIDTH + DA_V_WIDTH + 2 * RG_WIDTH + N_BRANCH * D_MODEL
SPLIT_AT = (DA_QK_WIDTH, 2 * DA_QK_WIDTH, 2 * DA_QK_WIDTH + DA_V_WIDTH,
            2 * DA_QK_WIDTH + DA_V_WIDTH + RG_WIDTH, 2 * DA_QK_WIDTH + DA_V_WIDTH + 2 * RG_WIDTH)
MEM_LEN = 256
CA_HEADS = 4
CA_DH = D_MODEL // CA_HEADS
PK_HEADS = 8
PK_DQ = 256
PK_DHALF = PK_DQ // 2
PK_NKEYS = 128
PK_EXPERTS = PK_NKEYS * PK_NKEYS
PK_TOPK = 16
PEER_BLOCK = 256

kernel_name = "hybrid_diffattn_rglru_peer_stream_step"


def rmsnorm(x, g, eps=RMS_EPS):
    xf = x.astype(jnp.float32)
    y = xf * lax.rsqrt(jnp.mean(xf * xf, axis=-1, keepdims=True) + eps)
    return (y * g.astype(jnp.float32)).astype(x.dtype)


def rope(x, pos):
    d = x.shape[-1]
    half = d // 2
    inv = jnp.exp(-math.log(ROPE_THETA) * jnp.arange(half, dtype=jnp.float32) * (2.0 / d))
    ang = pos.astype(jnp.float32)[:, None] * inv[None, :]
    cos = jnp.cos(ang)[None, :, None, :]
    sin = jnp.sin(ang)[None, :, None, :]
    xf = x.astype(jnp.float32)
    x1, x2 = xf[..., :half], xf[..., half:]
    return jnp.concatenate([x1 * cos - x2 * sin, x2 * cos + x1 * sin], axis=-1).astype(x.dtype)


def diff_attention(q, k, v, q_pos, k_pos, lam, g_subln, lam_init):
    b, lq = q.shape[0], q.shape[1]
    vf = v.astype(jnp.float32)
    kchunk = k_pos // CHUNK

    def block(qb, qpos):
        nq = qb.shape[1]
        s = jnp.einsum('bqhd,bkhd->bhqk', qb, k).astype(jnp.float32) * (DA_DK ** -0.5)
        mask = kchunk[None, :] <= (qpos // CHUNK)[:, None]
        s = jnp.where(mask[None, None], s, -jnp.inf)
        p = jax.nn.softmax(s, axis=-1).reshape(b, DA_HEADS, 2, nq, s.shape[-1])
        a = p[:, :, 0] - lam * p[:, :, 1]
        o = jnp.einsum('bhqk,bkhd->bqhd', a, vf)
        o = o * lax.rsqrt(jnp.mean(o * o, axis=-1, keepdims=True) + SUBLN_EPS)
        o = o * g_subln.astype(jnp.float32) * (1.0 - lam_init)
        return o.reshape(b, nq, DA_V_WIDTH).astype(v.dtype)

    if lq > Q_BLOCK and lq % Q_BLOCK == 0:
        nb = lq // Q_BLOCK
        qb = q.reshape(b, nb, Q_BLOCK, 2 * DA_HEADS, DA_DK).transpose(1, 0, 2, 3, 4)
        pb = q_pos.reshape(nb, Q_BLOCK)
        out = lax.map(lambda qp: block(qp[0], qp[1]), (qb, pb))
        return out.transpose(1, 0, 2, 3).reshape(b, lq, DA_V_WIDTH)
    return block(q, q_pos)


def _lin_combine(e1, e2):
    a1, b1 = e1
    a2, b2 = e2
    return a1 * a2, a2 * b1 + b2


def rglru_branch(xr, conv_buf, h0, conv_w, conv_b, wa, ba, wx, bx, lam):
    b, l, w = xr.shape
    xp = jnp.concatenate([conv_buf.astype(xr.dtype), xr], axis=1)
    c = conv_b[None, None, :] + conv_w[0][None, None, :] * xp[:, 0:l]
    for j in range(1, CONV_W):
        c = c + conv_w[j][None, None, :] * xp[:, j:j + l]
    new_buf = xp[:, l:]
    cb = c.reshape(b, l, RG_BLOCKS, RG_BW)
    r = jax.nn.sigmoid((jnp.einsum('blnc,ncd->blnd', cb, wa).reshape(b, l, w) + ba).astype(jnp.float32))
    i = jax.nn.sigmoid((jnp.einsum('blnc,ncd->blnd', cb, wx).reshape(b, l, w) + bx).astype(jnp.float32))
    log_a = -RG_C * r * jax.nn.softplus(-lam.astype(jnp.float32))
    a = jnp.exp(log_a)
    u = jnp.sqrt(-jnp.expm1(2.0 * log_a)) * i * c.astype(jnp.float32)
    a_cum, u_cum = lax.associative_scan(_lin_combine, (a, u), axis=1)
    h = a_cum * h0.astype(jnp.float32)[:, None, :] + u_cum
    return h.astype(xr.dtype), h[:, -1].astype(xr.dtype), new_buf


def memory_kv(mem, g_mem, w_ck, w_cv):
    b, m, _ = mem.shape
    mn = rmsnorm(mem, g_mem)
    return (mn @ w_ck).reshape(b, m, CA_HEADS, CA_DH), (mn @ w_cv).reshape(b, m, CA_HEADS, CA_DH)


def cross_attention(xn, mem_k, mem_v, w_cq, w_co):
    b, l, d = xn.shape
    q = (xn @ w_cq).reshape(b, l, CA_HEADS, CA_DH)
    s = jnp.einsum('blhd,bmhd->bhlm', q, mem_k).astype(jnp.float32) * (CA_DH ** -0.5)
    p = jax.nn.softmax(s, axis=-1)
    o = jnp.einsum('bhlm,bmhd->blhd', p, mem_v.astype(jnp.float32)).reshape(b, l, d).astype(xn.dtype)
    return o @ w_co


def peer(xn, w_pq, keys1, keys2, u_tab, v_tab):
    b, l, d = xn.shape
    t = b * l
    pad = (-t) % PEER_BLOCK
    xb = jnp.pad(xn.reshape(t, d), ((0, pad), (0, 0))).reshape(-1, PEER_BLOCK, d)
    k1 = keys1.astype(jnp.float32)
    k2 = keys2.astype(jnp.float32)

    def block(xt):
        q = (xt @ w_pq).astype(jnp.float32).reshape(PEER_BLOCK, PK_HEADS, 2, PK_DHALF)
        s1 = jnp.einsum('thd,hnd->thn', q[:, :, 0], k1)
        s2 = jnp.einsum('thd,hnd->thn', q[:, :, 1], k2)
        t1, i1 = lax.top_k(s1, PK_TOPK)
        t2, i2 = lax.top_k(s2, PK_TOPK)
        cand = (t1[..., :, None] + t2[..., None, :]).reshape(PEER_BLOCK, PK_HEADS, PK_TOPK * PK_TOPK)
        ts, ci = lax.top_k(cand, PK_TOPK)
        e = (jnp.take_along_axis(i1, ci // PK_TOPK, axis=-1) * PK_NKEYS
             + jnp.take_along_axis(i2, ci % PK_TOPK, axis=-1))
        g = jax.nn.softmax(ts, axis=-1)
        ue = u_tab[e]
        ve = v_tab[e]
        act = jax.nn.gelu(jnp.einsum('td,thkd->thk', xt, ue).astype(jnp.float32), approximate=False)
        return jnp.einsum('thk,thkd->td', (g * act).astype(ve.dtype), ve)

    out = lax.map(block, xb).reshape(-1, d)[:t]
    return out.reshape(b, l, d).astype(xn.dtype)


def layer(x, pos, k_past, v_past, conv_buf, h0, mem_k, mem_v, params, lam_init):
    (g_mix, w_in, lam_q1, lam_k1, lam_q2, lam_k2, g_subln, conv_w, conv_b, rg_wa, rg_ba,
     rg_wx, rg_bx, rg_lambda, w_out, g_cross, w_cq, w_co, g_ffn, w_pq, pk_keys1, pk_keys2,
     peer_u, peer_v) = params
    b, l, d = x.shape
    n = rmsnorm(x, g_mix)
    z = n @ w_in
    q, k, v, rx, rg, gates = jnp.split(z, SPLIT_AT, axis=-1)
    q = rope(q.reshape(b, l, 2 * DA_HEADS, DA_DK), pos)
    k = rope(k.reshape(b, l, 2 * DA_HEADS, DA_DK), pos)
    v = v.reshape(b, l, DA_HEADS, DA_DV)
    k_all = jnp.concatenate([k_past.astype(k.dtype), k], axis=1)
    v_all = jnp.concatenate([v_past.astype(v.dtype), v], axis=1)
    k_pos = jnp.concatenate([jnp.arange(k_past.shape[1], dtype=jnp.int32), pos])
    lam = (jnp.exp(jnp.sum(lam_q1.astype(jnp.float32) * lam_k1.astype(jnp.float32)))
           - jnp.exp(jnp.sum(lam_q2.astype(jnp.float32) * lam_k2.astype(jnp.float32))) + lam_init)
    y_att = diff_attention(q, k_all, v_all, pos, k_pos, lam, g_subln, lam_init)
    h, h_last, new_buf = rglru_branch(rx, conv_buf, h0, conv_w, conv_b, rg_wa, rg_ba, rg_wx, rg_bx, rg_lambda)
    y_rec = jax.nn.gelu(rg, approximate=False) * h
    gs = jax.nn.sigmoid(gates.astype(jnp.float32)).reshape(b, l, N_BRANCH, d)
    merged = (gs[:, :, 0] * y_att.astype(jnp.float32) + gs[:, :, 1] * y_rec.astype(jnp.float32)).astype(x.dtype)
    x = x + merged @ w_out
    x = x + cross_attention(rmsnorm(x, g_cross), mem_k, mem_v, w_cq, w_co)
    x = x + peer(rmsnorm(x, g_ffn), w_pq, pk_keys1, pk_keys2, peer_u, peer_v)
    return x, k, v, new_buf, h_last


def setup_inputs(seed: int = 0) -> dict:
    key = jax.random.key(seed)
    ks = iter(jax.random.split(key, 48))
    f32 = jnp.float32

    def nrm(shape, scale):
        return jax.random.normal(next(ks), shape, f32) * scale

    def gain(shape):
        return 1.0 + 0.05 * jax.random.normal(next(ks), shape, f32)

    aq = jax.random.uniform(next(ks), (DEPTH, RG_WIDTH), f32, minval=0.9, maxval=0.999)
    a_base = aq ** (1.0 / RG_C)
    rg_lambda = jnp.log(a_base) - jnp.log1p(-a_base)
    return {
        'x_prompt': nrm((BATCH, SEQ, D_MODEL), 1.0),
        'x_sample': nrm((DEC_BATCH, DEC_SEQ, D_MODEL), 1.0),
        'cache_attn_k': nrm((DEPTH, DEC_BATCH, PAST_LEN, 2 * DA_HEADS, DA_DK), 1.0),
        'cache_attn_v': nrm((DEPTH, DEC_BATCH, PAST_LEN, DA_HEADS, DA_DV), 1.0),
        'cache_mem_k': nrm((DEPTH, DEC_BATCH, MEM_LEN, CA_HEADS, CA_DH), 1.0),
        'cache_mem_v': nrm((DEPTH, DEC_BATCH, MEM_LEN, CA_HEADS, CA_DH), 1.0),
        'state_conv': nrm((DEPTH, DEC_BATCH, CONV_W - 1, RG_WIDTH), 1.0),
        'state_rglru': nrm((DEPTH, DEC_BATCH, RG_WIDTH), 0.5),
        'mem_prompt': nrm((BATCH, MEM_LEN, D_MODEL), 1.0),
        'g_mix': gain((DEPTH, D_MODEL)),
        'w_in': nrm((DEPTH, D_MODEL, IN_COLS), D_MODEL ** -0.5),
        'lam_q1': nrm((DEPTH, DA_DK), 0.1),
        'lam_k1': nrm((DEPTH, DA_DK), 0.1),
        'lam_q2': nrm((DEPTH, DA_DK), 0.1),
        'lam_k2': nrm((DEPTH, DA_DK), 0.1),
        'g_subln': gain((DEPTH, DA_DV)),
        'conv_w': nrm((DEPTH, CONV_W, RG_WIDTH), CONV_W ** -0.5),
        'conv_b': nrm((DEPTH, RG_WIDTH), 0.02),
        'rg_wa': nrm((DEPTH, RG_BLOCKS, RG_BW, RG_BW), RG_BW ** -0.5),
        'rg_ba': nrm((DEPTH, RG_WIDTH), 0.02),
        'rg_wx': nrm((DEPTH, RG_BLOCKS, RG_BW, RG_BW), RG_BW ** -0.5),
        'rg_bx': nrm((DEPTH, RG_WIDTH), 0.02),
        'rg_lambda': rg_lambda,
        'w_out': nrm((DEPTH, D_MODEL, D_MODEL), D_MODEL ** -0.5),
        'g_cross': gain((DEPTH, D_MODEL)),
        'g_mem': gain((DEPTH, D_MODEL)),
        'w_cq': nrm((DEPTH, D_MODEL, D_MODEL), D_MODEL ** -0.5),
        'w_ck': nrm((DEPTH, D_MODEL, D_MODEL), D_MODEL ** -0.5),
        'w_cv': nrm((DEPTH, D_MODEL, D_MODEL), D_MODEL ** -0.5),
        'w_co': nrm((DEPTH, D_MODEL, D_MODEL), D_MODEL ** -0.5),
        'g_ffn': gain((DEPTH, D_MODEL)),
        'w_pq': nrm((DEPTH, D_MODEL, PK_HEADS * PK_DQ), D_MODEL ** -0.5),
        'pk_keys1': nrm((DEPTH, PK_HEADS, PK_NKEYS, PK_DHALF), PK_DHALF ** -0.5),
        'pk_keys2': nrm((DEPTH, PK_HEADS, PK_NKEYS, PK_DHALF), PK_DHALF ** -0.5),
        'peer_u': nrm((DEPTH, PK_EXPERTS, D_MODEL), D_MODEL ** -0.5),
        'peer_v': nrm((DEPTH, PK_EXPERTS, D_MODEL), PK_HEADS ** -0.5),
        'g_final': gain((D_MODEL,)),
    }


def reference(x_prompt, x_sample, cache_attn_k, cache_attn_v, cache_mem_k, cache_mem_v, state_conv,
              state_rglru, mem_prompt, g_mix, w_in, lam_q1, lam_k1, lam_q2, lam_k2, g_subln, conv_w,
              conv_b, rg_wa, rg_ba, rg_wx, rg_bx, rg_lambda, w_out, g_cross, g_mem, w_cq, w_ck, w_cv,
              w_co, g_ffn, w_pq, pk_keys1, pk_keys2, peer_u, peer_v, g_final):
    b, l = x_prompt.shape[0], x_prompt.shape[1]
    db, dl = x_sample.shape[0], x_sample.shape[1]
    past = cache_attn_k.shape[2]
    pos_p = jnp.arange(l, dtype=jnp.int32)
    pos_s = past + jnp.arange(dl, dtype=jnp.int32)
    xp, xs = x_prompt, x_sample
    kp_l, vp_l, mk_l, mv_l, cp_l, hp_l = [], [], [], [], [], []
    ks_l, vs_l, cs_l, hs_l = [], [], [], []
    for li in range(DEPTH):
        lam_init = 0.8 - 0.6 * math.exp(-0.3 * li)
        params = (g_mix[li], w_in[li], lam_q1[li], lam_k1[li], lam_q2[li], lam_k2[li], g_subln[li],
                  conv_w[li], conv_b[li], rg_wa[li], rg_ba[li], rg_wx[li], rg_bx[li], rg_lambda[li],
                  w_out[li], g_cross[li], w_cq[li], w_co[li], g_ffn[li], w_pq[li], pk_keys1[li],
                  pk_keys2[li], peer_u[li], peer_v[li])
        mk_p, mv_p = memory_kv(mem_prompt, g_mem[li], w_ck[li], w_cv[li])
        xp, kp, vp, cp, hp = layer(
            xp, pos_p,
            jnp.zeros((b, 0, 2 * DA_HEADS, DA_DK), xp.dtype), jnp.zeros((b, 0, DA_HEADS, DA_DV), xp.dtype),
            jnp.zeros((b, CONV_W - 1, RG_WIDTH), xp.dtype), jnp.zeros((b, RG_WIDTH), xp.dtype),
            mk_p, mv_p, params, lam_init)
        xs, ks_, vs_, cs, hs = layer(
            xs, pos_s, cache_attn_k[li], cache_attn_v[li], state_conv[li], state_rglru[li],
            cache_mem_k[li], cache_mem_v[li], params, lam_init)
        kp_l.append(kp); vp_l.append(vp); mk_l.append(mk_p); mv_l.append(mv_p)
        cp_l.append(cp); hp_l.append(hp)
        ks_l.append(ks_); vs_l.append(vs_); cs_l.append(cs); hs_l.append(hs)
    y_prompt = rmsnorm(xp, g_final)
    y_sample = rmsnorm(xs, g_final)
    return (y_prompt, y_sample,
            jnp.stack(kp_l), jnp.stack(vp_l), jnp.stack(mk_l), jnp.stack(mv_l),
            jnp.stack(cp_l), jnp.stack(hp_l),
            jnp.stack(ks_l), jnp.stack(vs_l), jnp.stack(cs_l), jnp.stack(hs_l))
```

```python
import functools
import math

import jax
import jax.numpy as jnp
from jax import lax
from jax.experimental import pallas as pl
from jax.experimental.pallas import tpu as pltpu

F32 = jnp.float32
BF16 = jnp.bfloat16

D_MODEL = 1024
CHUNK = 64
RMS_EPS = 1e-6
SUBLN_EPS = 1e-5
ROPE_THETA = 10000.0
DA_HEADS = 8
DA_DK = 64
DA_DV = 2 * DA_DK
RG_WIDTH = D_MODEL
RG_BLOCKS = 16
RG_BW = RG_WIDTH // RG_BLOCKS
RG_C = 8.0
CONV_W = 4
IN_SECTIONS = 7
IN_COLS = IN_SECTIONS * D_MODEL
MEM_LEN = 256
CA_HEADS = 4
CA_DH = D_MODEL // CA_HEADS
PK_HEADS = 8
PK_DQ = 256
PK_DHALF = PK_DQ // 2
PK_NKEYS = 128
PK_EXPERTS = PK_NKEYS * PK_NKEYS
PK_TOPK = 16
LAM_INIT = 0.8 - 0.6 * math.exp(-0.3 * 0)

LANES = 128
SUBLANES = 8
RG_GROUP = 256
NEG_BIG = -1e30

PAIR_COLS = tuple(PK_TOPK // (a + 1) for a in range(PK_TOPK))


def _cparams(semantics, vmem_mib):
    return pltpu.CompilerParams(dimension_semantics=semantics, vmem_limit_bytes=vmem_mib << 20)


def _rmsnorm(x, g, eps=RMS_EPS):
    return x * lax.rsqrt(jnp.mean(x * x, axis=-1, keepdims=True) + eps) * g


def _gelu(x):
    return 0.5 * x * (1.0 + lax.erf(x * (2.0 ** -0.5)))


def _sigmoid(x):
    return 1.0 / (1.0 + jnp.exp(-x))


def _dot(a, b):
    return jnp.dot(a, b, preferred_element_type=F32)


def _dot_nt(a, b):
    return lax.dot_general(a, b, (((1,), (1,)), ((), ())), preferred_element_type=F32)


def _inproj_kernel(x_ref, g_ref, w_ref, cos_ref, sin_ref,
                   q_ref, k_ref, v_ref, rx_ref, grg_ref, sga_ref, sgr_ref):
    n = _rmsnorm(x_ref[...], g_ref[...]).astype(BF16)
    cos = cos_ref[...]
    sin = sin_ref[...]
    lane = lax.broadcasted_iota(jnp.int32, cos.shape, 1)
    first_half = (lane & (DA_DK - 1)) < (DA_DK // 2)

    def proj(s):
        return _dot(n, w_ref[:, s * D_MODEL:(s + 1) * D_MODEL])

    def rope_cols(z, c):
        zc = z[:, c * LANES:(c + 1) * LANES]
        partner = jnp.where(first_half, pltpu.roll(zc, LANES - DA_DK // 2, 1),
                            pltpu.roll(zc, DA_DK // 2, 1))
        return zc * cos + partner * sin

    zq = proj(0)
    for c in range(D_MODEL // LANES):
        q_ref[:, c * LANES:(c + 1) * LANES] = (rope_cols(zq, c) * (DA_DK ** -0.5)).astype(BF16)
    zk = proj(1)
    for c in range(D_MODEL // LANES):
        k_ref[:, c * LANES:(c + 1) * LANES] = rope_cols(zk, c)
    v_ref[...] = proj(2)
    rx_ref[...] = proj(3)
    grg_ref[...] = _gelu(proj(4))
    sga_ref[...] = _sigmoid(proj(5))
    sgr_ref[...] = _sigmoid(proj(6))


def _inproj_call(x, g_mix, w_in, cos_t, sin_t, tm):
    t = x.shape[0]
    period = cos_t.shape[0] // tm
    row = lambda i: (i, 0)
    fixed = lambda i: (0, 0)
    f32_out = jax.ShapeDtypeStruct((t, D_MODEL), F32)
    return pl.pallas_call(
        _inproj_kernel,
        grid=(t // tm,),
        in_specs=[pl.BlockSpec((tm, D_MODEL), row),
                  pl.BlockSpec((1, D_MODEL), fixed),
                  pl.BlockSpec((D_MODEL, IN_COLS), fixed),
                  pl.BlockSpec((tm, LANES), lambda i: (i % period, 0)),
                  pl.BlockSpec((tm, LANES), lambda i: (i % period, 0))],
        out_specs=[pl.BlockSpec((tm, D_MODEL), row)] * 7,
        out_shape=[jax.ShapeDtypeStruct((t, D_MODEL), BF16)] + [f32_out] * 6,
        compiler_params=_cparams(("parallel",), 56),
        name="inproj",
    )(x, g_mix, w_in, cos_t, sin_t)


def _rglru_kernel(rx_ref, grg_ref, cst_ref, h0_ref, cw_ref, cb_ref, wbd_ref, bax_ref, lam_ref,
                  yrec_ref, hlast_ref, xbuf, hcar, *, tl):
    tb = pl.program_id(1)

    @pl.when(tb == 0)
    def _():
        xbuf[0:SUBLANES, :] = cst_ref[...]
        hcar[...] = h0_ref[...]

    x = rx_ref[...]
    xbuf[SUBLANES:SUBLANES + tl, :] = x
    cw = cw_ref[...]
    c = cb_ref[...] + cw[CONV_W - 1:CONV_W, :] * x
    for j in range(CONV_W - 1):
        back = CONV_W - 1 - j
        c = c + cw[j:j + 1, :] * xbuf[SUBLANES - back:SUBLANES - back + tl, :]
    xbuf[0:SUBLANES, :] = xbuf[tl:tl + SUBLANES, :]

    lam = lam_ref[...]
    softplus_neg = jnp.maximum(-lam, 0.0) + jnp.log1p(jnp.exp(-jnp.abs(lam)))
    cb16 = c.astype(BF16)
    a_parts, u_parts = [], []
    for g in range(RG_WIDTH // RG_GROUP):
        cols = slice(g * RG_GROUP, (g + 1) * RG_GROUP)
        z = _dot(cb16[:, cols], wbd_ref[g])
        r = _sigmoid(z[:, :RG_GROUP] + bax_ref[0:1, cols])
        i = _sigmoid(z[:, RG_GROUP:] + bax_ref[1:2, cols])
        log_a = (-RG_C) * r * softplus_neg[:, cols]
        th = jnp.tanh(log_a)
        a_parts.append(jnp.exp(log_a))
        u_parts.append(jnp.sqrt(-2.0 * th / (1.0 - th)) * i * c[:, cols])
    a = jnp.concatenate(a_parts, axis=1)
    u = jnp.concatenate(u_parts, axis=1)

    rowid = lax.broadcasted_iota(jnp.int32, a.shape, 0)
    k = 1
    while k < tl:
        keep = rowid >= k
        a_prev = jnp.where(keep, pltpu.roll(a, k, 0), 1.0)
        u_prev = jnp.where(keep, pltpu.roll(u, k, 0), 0.0)
        u = a * u_prev + u
        a = a * a_prev
        k *= 2
    h = a * hcar[...] + u
    hcar[...] = h[tl - 1:tl, :]
    yrec_ref[...] = grg_ref[...] * h

    @pl.when(tb == pl.num_programs(1) - 1)
    def _():
        hlast_ref[...] = h[tl - 1:tl, :]


def _rglru_call(rx, grg, conv_state8, h0, conv_w, conv_b, wbd, bax, lam, nb, lb, tl):
    w = RG_WIDTH
    nt = lb // tl
    fixed2 = lambda b, t: (0, 0)
    yrec, hlast = pl.pallas_call(
        functools.partial(_rglru_kernel, tl=tl),
        grid=(nb, nt),
        in_specs=[pl.BlockSpec((tl, w), lambda b, t: (b * nt + t, 0)),
                  pl.BlockSpec((tl, w), lambda b, t: (b * nt + t, 0)),
                  pl.BlockSpec((None, SUBLANES, w), lambda b, t: (b, 0, 0)),
                  pl.BlockSpec((None, 1, w), lambda b, t: (b, 0, 0)),
                  pl.BlockSpec((CONV_W, w), fixed2),
                  pl.BlockSpec((1, w), fixed2),
                  pl.BlockSpec((w // RG_GROUP, RG_GROUP, 2 * RG_GROUP), lambda b, t: (0, 0, 0)),
                  pl.BlockSpec((2, w), fixed2),
                  pl.BlockSpec((1, w), fixed2)],
        out_specs=[pl.BlockSpec((tl, w), lambda b, t: (b * nt + t, 0)),
                   pl.BlockSpec((None, 1, w), lambda b, t: (b, 0, 0))],
        out_shape=[jax.ShapeDtypeStruct((nb * lb, w), F32),
                   jax.ShapeDtypeStruct((nb, 1, w), F32)],
        scratch_shapes=[pltpu.VMEM((tl + SUBLANES, w), F32), pltpu.VMEM((1, w), F32)],
        compiler_params=_cparams(("parallel", "arbitrary"), 40),
        name="rglru",
    )(rx, grg, conv_state8, h0, conv_w, conv_b, wbd, bax, lam)
    return yrec, hlast


def _attn_kernel(*refs, tq, tk, n_past, tkp):
    if n_past:
        (lq1, lk1, lq2, lk2, gsub_ref, q_ref, k_ref, v_ref, kp_ref, vp_ref, o_ref,
         m_s, l_s, acc_s) = refs
    else:
        (lq1, lk1, lq2, lk2, gsub_ref, q_ref, k_ref, v_ref, o_ref, m_s, l_s, acc_s) = refs
        kp_ref = vp_ref = None
    qi = pl.program_id(2)
    q = q_ref[...]
    lane = lax.broadcasted_iota(jnp.int32, q.shape, 1)
    zero = jnp.zeros_like(q)
    q_streams = (jnp.where(lane < DA_DK, q, zero), jnp.where(lane >= DA_DK, q, zero))

    m_s[...] = jnp.full(m_s.shape, NEG_BIG, F32)
    l_s[...] = jnp.zeros(l_s.shape, F32)
    acc_s[...] = jnp.zeros(acc_s.shape, F32)

    def update(kb, vb, mask):
        for st in range(2):
            s = _dot_nt(q_streams[st], kb)
            if mask is not None:
                s = jnp.where(mask, s, NEG_BIG)
            m_old = m_s[st]
            m_new = jnp.maximum(m_old, jnp.max(s, axis=1, keepdims=True))
            alpha = jnp.exp(m_old - m_new)
            p = jnp.exp(s - m_new)
            l_s[st] = alpha * l_s[st] + jnp.sum(p, axis=1, keepdims=True)
            acc_s[st] = alpha * acc_s[st] + _dot(p.astype(BF16), vb)
            m_s[st] = m_new

    if n_past:
        def past_body(j, carry):
            start = pl.multiple_of(j * tkp, tkp)
            update(kp_ref[pl.ds(start, tkp), :].astype(BF16),
                   vp_ref[pl.ds(start, tkp), :].astype(BF16), None)
            return carry
        lax.fori_loop(0, n_past, past_body, 0)

    def self_body(j, carry):
        start = pl.multiple_of(j * tk, tk)
        update(k_ref[pl.ds(start, tk), :].astype(BF16), v_ref[pl.ds(start, tk), :].astype(BF16), None)
        return carry
    lax.fori_loop(0, qi, self_body, 0)

    start = pl.multiple_of(qi * tk, tk)
    qchunk = lax.broadcasted_iota(jnp.int32, (tq, tk), 0) // CHUNK
    kchunk = lax.broadcasted_iota(jnp.int32, (tq, tk), 1) // CHUNK
    update(k_ref[pl.ds(start, tk), :].astype(BF16), v_ref[pl.ds(start, tk), :].astype(BF16),
           kchunk <= qchunk)

    lam = (jnp.exp(jnp.sum(lq1[...] * lk1[...], axis=1, keepdims=True))
           - jnp.exp(jnp.sum(lq2[...] * lk2[...], axis=1, keepdims=True)) + LAM_INIT)
    o = acc_s[0] / l_s[0] - lam * (acc_s[1] / l_s[1])
    o = o * lax.rsqrt(jnp.mean(o * o, axis=1, keepdims=True) + SUBLN_EPS)
    o_ref[...] = o * gsub_ref[...] * (1.0 - LAM_INIT)


def _attn_call(q, k, v, lam_params, g_subln, nb, lb, tq, k_past=None, v_past=None, tkp=512):
    w = D_MODEL
    nq = lb // tq
    vec = lambda b, h, i: (0, 0)
    in_specs = [pl.BlockSpec((1, DA_DK), vec)] * 4 + [
        pl.BlockSpec((1, DA_DV), vec),
        pl.BlockSpec((tq, DA_DV), lambda b, h, i: (b * nq + i, h)),
        pl.BlockSpec((None, lb, DA_DV), lambda b, h, i: (b, 0, h)),
        pl.BlockSpec((None, lb, DA_DV), lambda b, h, i: (b, 0, h))]
    args = list(lam_params) + [g_subln, q, k, v]
    n_past = 0
    if k_past is not None:
        p = k_past.shape[1]
        tkp = _pick(p, tkp)
        n_past = p // tkp
        in_specs += [pl.BlockSpec((None, p, DA_DV), lambda b, h, i: (b, 0, h))] * 2
        args += [k_past, v_past]
    return pl.pallas_call(
        functools.partial(_attn_kernel, tq=tq, tk=tq, n_past=n_past, tkp=tkp),
        grid=(nb, DA_HEADS, nq),
        in_specs=in_specs,
        out_specs=pl.BlockSpec((tq, DA_DV), lambda b, h, i: (b * nq + i, h)),
        out_shape=jax.ShapeDtypeStruct((nb * lb, w), F32),
        scratch_shapes=[pltpu.VMEM((2, tq, 1), F32), pltpu.VMEM((2, tq, 1), F32),
                        pltpu.VMEM((2, tq, DA_DV), F32)],
        compiler_params=_cparams(("parallel", "parallel", "arbitrary"), 40),
        name="diff_attn",
    )(*args)


def _memkv_kernel(m_ref, g_ref, wk_ref, wv_ref, k_ref, v_ref):
    mn = _rmsnorm(m_ref[...], g_ref[...]).astype(BF16)
    k_ref[...] = _dot(mn, wk_ref[...])
    v_ref[...] = _dot(mn, wv_ref[...])


def _memkv_call(mem, g_mem, w_ck, w_cv, tm):
    t = mem.shape[0]
    row = lambda i: (i, 0)
    fixed = lambda i: (0, 0)
    return pl.pallas_call(
        _memkv_kernel,
        grid=(t // tm,),
        in_specs=[pl.BlockSpec((tm, D_MODEL), row), pl.BlockSpec((1, D_MODEL), fixed),
                  pl.BlockSpec((D_MODEL, D_MODEL), fixed), pl.BlockSpec((D_MODEL, D_MODEL), fixed)],
        out_specs=[pl.BlockSpec((tm, D_MODEL), row)] * 2,
        out_shape=[jax.ShapeDtypeStruct((t, D_MODEL), F32)] * 2,
        compiler_params=_cparams(("parallel",), 32),
        name="memory_kv",
    )(mem, g_mem, w_ck, w_cv)


def _mid_kernel(x_ref, ya_ref, yr_ref, sga_ref, sgr_ref, wo_ref, gc_ref, wcq_ref, mk_ref, mv_ref,
                wco_ref, gf_ref, wpq_ref, x2_ref, pq_ref, *, groups, rows):
    merged = (sga_ref[...] * ya_ref[...] + sgr_ref[...] * yr_ref[...]).astype(BF16)
    x1 = x_ref[...] + _dot(merged, wo_ref[...])
    qc = _dot(_rmsnorm(x1, gc_ref[...]).astype(BF16), wcq_ref[...]).astype(BF16)
    o_rows = []
    for g in range(groups):
        mk = mk_ref[g].astype(BF16)
        mv = mv_ref[g].astype(BF16)
        o_heads = []
        for h in range(CA_HEADS):
            cols = slice(h * CA_DH, (h + 1) * CA_DH)
            s = _dot_nt(qc[g * rows:(g + 1) * rows, cols], mk[:, cols]) * (CA_DH ** -0.5)
            p = jnp.exp(s - jnp.max(s, axis=1, keepdims=True))
            p = p / jnp.sum(p, axis=1, keepdims=True)
            o_heads.append(_dot(p.astype(BF16), mv[:, cols]))
        o_rows.append(jnp.concatenate(o_heads, axis=1))
    o = o_rows[0] if groups == 1 else jnp.concatenate(o_rows, axis=0)
    x2 = x1 + _dot(o.astype(BF16), wco_ref[...])
    x2_ref[...] = x2
    pq_ref[...] = _dot(_rmsnorm(x2, gf_ref[...]).astype(BF16), wpq_ref[...]).astype(BF16)


def _mid_call(x, y_att, y_rec, sga, sgr, w_out, g_cross, w_cq, mem_k, mem_v, w_co, g_ffn, w_pq,
              lb, tm):
    t = x.shape[0]
    rows = min(tm, lb)
    groups = tm // rows
    per_mem = lb // rows
    row = lambda i: (i, 0)
    fixed = lambda i: (0, 0)
    mem_spec = pl.BlockSpec((groups, MEM_LEN, D_MODEL), lambda i: (i // per_mem, 0, 0))
    act = pl.BlockSpec((tm, D_MODEL), row)
    wsq = pl.BlockSpec((D_MODEL, D_MODEL), fixed)
    gvec = pl.BlockSpec((1, D_MODEL), fixed)
    return pl.pallas_call(
        functools.partial(_mid_kernel, groups=groups, rows=rows),
        grid=(t // tm,),
        in_specs=[act, act, act, act, act, wsq, gvec, wsq, mem_spec, mem_spec, wsq, gvec,
                  pl.BlockSpec((D_MODEL, PK_HEADS * PK_DQ), fixed)],
        out_specs=[act, pl.BlockSpec((tm, PK_HEADS * PK_DQ), row)],
        out_shape=[jax.ShapeDtypeStruct((t, D_MODEL), F32),
                   jax.ShapeDtypeStruct((t, PK_HEADS * PK_DQ), BF16)],
        compiler_params=_cparams(("parallel",), 56),
        name="merge_cross_attn",
    )(x, y_att, y_rec, sga, sgr, w_out, g_cross, w_cq, mem_k, mem_v, w_co, g_ffn, w_pq)


def _extract_top(s, order, t_ref):
    def body(r, carry):
        s, rank = carry
        m = jnp.max(s, axis=0, keepdims=True)
        first = jnp.min(jnp.where(s == m, order, 1e9), axis=0, keepdims=True)
        sel = order == first
        t_ref[pl.ds(r, 1), :] = m
        return jnp.where(sel, -jnp.inf, s), jnp.where(sel, r.astype(F32), rank)
    _, rank = lax.fori_loop(0, PK_TOPK, body, (s, jnp.full(s.shape, float(PK_TOPK), F32)))
    return rank


def _route_head(pq_ref, k1_ref, k2_ref, h, n1_s, p_s, r2_s, q_s, t1_s, t2_s, ts_s, na_s):
    tb = pq_ref.shape[0]
    s1 = _dot_nt(k1_ref[h], pq_ref[:, (2 * h) * PK_DHALF:(2 * h + 1) * PK_DHALF])
    s2 = _dot_nt(k2_ref[h], pq_ref[:, (2 * h + 1) * PK_DHALF:(2 * h + 2) * PK_DHALF])
    key_id = lax.broadcasted_iota(jnp.int32, (PK_NKEYS, tb), 0).astype(F32)
    rank1 = _extract_top(s1, key_id, t1_s)
    rank2 = _extract_top(s2, key_id, t2_s)
    t1 = t1_s[...]
    t2 = t2_s[...]

    row16 = lax.broadcasted_iota(jnp.int32, (PK_TOPK, tb), 0).astype(F32)
    row8 = lax.broadcasted_iota(jnp.int32, (SUBLANES, tb), 0).astype(F32)
    vals = [t1[0:1] + t2]
    order = [row16]
    for a in range(1, SUBLANES):
        vals.append(jnp.where(row8 < PAIR_COLS[a], t1[a:a + 1] + t2[:SUBLANES], -jnp.inf))
        order.append(row8 + float(a * PK_TOPK))
    vals.append(t1[SUBLANES:] + t2[0:1])
    order.append((row8 + float(SUBLANES)) * float(PK_TOPK))
    rank_c = _extract_top(jnp.concatenate(vals, axis=0), jnp.concatenate(order, axis=0), ts_s)
    picked = jnp.where(rank_c < PK_TOPK, 1.0, 0.0)
    na_s[0:1, :] = jnp.sum(picked[0:PK_TOPK], axis=0, keepdims=True)
    for a in range(1, SUBLANES):
        lo = PK_TOPK + (a - 1) * SUBLANES
        na_s[a:a + 1, :] = jnp.sum(picked[lo:lo + SUBLANES], axis=0, keepdims=True)
    na_s[SUBLANES:, :] = picked[PK_TOPK + (SUBLANES - 1) * SUBLANES:]
    ts = ts_s[...]
    z = jnp.sum(jnp.exp(ts - ts[0:1]), axis=0, keepdims=True)

    n1 = jnp.zeros((PK_NKEYS, tb), F32)
    for a in range(PK_TOPK):
        n1 = jnp.where(rank1 == float(a), na_s[a:a + 1, :], n1)
    n1_s[h] = n1
    p_s[h] = jnp.exp(s1 - t1[0:1])
    r2_s[h] = rank2
    q_s[h] = jnp.exp(s2 - t2[0:1]) / z


def _peer_kernel(x2_ref, pq_ref, gf_ref, gfin_ref, k1_ref, k2_ref, u_ref, vt_ref, y_ref,
                 xf_s, n1_s, p_s, r2_s, q_s, t1_s, t2_s, ts_s, na_s, wa_s, acc_s, *, ni):
    e = pl.program_id(1)

    @pl.when(e == 0)
    def _():
        xf_s[...] = _rmsnorm(x2_ref[...], gf_ref[...]).astype(BF16)
        acc_s[...] = jnp.zeros(acc_s.shape, F32)
        for h in range(PK_HEADS):
            _route_head(pq_ref, k1_ref, k2_ref, h, n1_s, p_s, r2_s, q_s, t1_s, t2_s, ts_s, na_s)

    act_t = _dot_nt(u_ref[...], xf_s[...])
    for ii in range(ni):
        i = e * ni + ii
        w = jnp.zeros((PK_NKEYS, act_t.shape[1]), F32)
        for h in range(PK_HEADS):
            picked = r2_s[h] < n1_s[h, pl.ds(i, 1), :]
            w = w + jnp.where(picked, q_s[h], 0.0) * p_s[h, pl.ds(i, 1), :]
        rows = slice(ii * PK_NKEYS, (ii + 1) * PK_NKEYS)
        wa_s[rows, :] = (w * _gelu(act_t[rows, :])).astype(BF16)
    acc_s[...] += _dot(vt_ref[...], wa_s[...])

    @pl.when(e == pl.num_programs(1) - 1)
    def _():
        y_ref[...] = _rmsnorm(x2_ref[...] + acc_s[...].T, gfin_ref[...])


def _peer_call(x2, pq, g_ffn, g_final, keys1, keys2, u_tab, v_tab_t, tb, ni):
    t = x2.shape[0]
    eb = ni * PK_NKEYS
    fixed = lambda i, e: (0, 0)
    keys_spec = pl.BlockSpec((PK_HEADS, PK_NKEYS, PK_DHALF), lambda i, e: (0, 0, 0))
    route = pltpu.VMEM((PK_HEADS, PK_NKEYS, tb), F32)
    top = pltpu.VMEM((PK_TOPK, tb), F32)
    return pl.pallas_call(
        functools.partial(_peer_kernel, ni=ni),
        grid=(t // tb, PK_EXPERTS // eb),
        in_specs=[pl.BlockSpec((tb, D_MODEL), lambda i, e: (i, 0)),
                  pl.BlockSpec((tb, PK_HEADS * PK_DQ), lambda i, e: (i, 0)),
                  pl.BlockSpec((1, D_MODEL), fixed), pl.BlockSpec((1, D_MODEL), fixed),
                  keys_spec, keys_spec,
                  pl.BlockSpec((eb, D_MODEL), lambda i, e: (e, 0)),
                  pl.BlockSpec((D_MODEL, eb), lambda i, e: (0, e))],
        out_specs=pl.BlockSpec((tb, D_MODEL), lambda i, e: (i, 0)),
        out_shape=jax.ShapeDtypeStruct((t, D_MODEL), F32),
        scratch_shapes=[pltpu.VMEM((tb, D_MODEL), BF16), route, route, route, route,
                        top, top, top, top,
                        pltpu.VMEM((eb, tb), BF16), pltpu.VMEM((D_MODEL, tb), F32)],
        compiler_params=_cparams(("parallel", "arbitrary"), 56),
        name="peer",
    )(x2, pq, g_ffn, g_final, keys1, keys2, u_tab, v_tab_t)


def _rope_tables(pos, rows):
    half = DA_DK // 2
    inv = jnp.exp(-math.log(ROPE_THETA) * jnp.arange(half, dtype=F32) * (2.0 / DA_DK))
    ang = pos.astype(F32)[:, None] * inv[None, :]
    cos = jnp.tile(jnp.cos(ang), (1, LANES // half))
    sin = jnp.tile(jnp.concatenate([-jnp.sin(ang), jnp.sin(ang)], axis=1), (1, LANES // DA_DK))
    reps = max(1, rows // pos.shape[0])
    return jnp.tile(cos, (reps, 1)), jnp.tile(sin, (reps, 1))


def _block_diag_gates(wa, wx):
    per = RG_GROUP // RG_BW
    def bd(w):
        w = w.reshape(RG_WIDTH // RG_GROUP, per, RG_BW, RG_BW)
        eye = jnp.eye(per, dtype=w.dtype)
        return jnp.einsum("gmcd,mn->gmcnd", w, eye).reshape(RG_WIDTH // RG_GROUP, RG_GROUP, RG_GROUP)
    return jnp.concatenate([bd(wa), bd(wx)], axis=2).astype(BF16)


def _group_forward(x, pos, past_kv, conv_state, h0, mem_kv, wts, tiles):
    nb, lb, d = x.shape
    t = nb * lb
    tm, tl, tq, tmid, tb, ni = tiles
    xf = x.reshape(t, d)
    cos_t, sin_t = _rope_tables(pos, tm)
    q, k, v, rx, grg, sga, sgr = _inproj_call(xf, wts["g_mix"], wts["w_in"], cos_t, sin_t, tm)

    conv8 = jnp.concatenate(
        [jnp.zeros((nb, SUBLANES - (CONV_W - 1), RG_WIDTH), F32), conv_state.astype(F32)], axis=1)
    y_rec, h_last = _rglru_call(rx, grg, conv8, h0.reshape(nb, 1, RG_WIDTH), wts["conv_w"],
                                wts["conv_b"], wts["wbd"], wts["bax"], wts["rg_lambda"], nb, lb, tl)
    new_conv = jnp.concatenate([conv_state.astype(F32), rx.reshape(nb, lb, RG_WIDTH)],
                               axis=1)[:, lb:]

    k3 = k.reshape(nb, lb, d)
    v3 = v.reshape(nb, lb, d)
    if past_kv is None:
        y_att = _attn_call(q, k3, v3, wts["lam"], wts["g_subln"], nb, lb, tq)
    else:
        y_att = _attn_call(q, k3, v3, wts["lam"], wts["g_subln"], nb, lb, tq,
                           k_past=past_kv[0], v_past=past_kv[1])

    x2, pq = _mid_call(xf, y_att, y_rec, sga, sgr, wts["w_out"], wts["g_cross"], wts["w_cq"],
                       mem_kv[0], mem_kv[1], wts["w_co"], wts["g_ffn"], wts["w_pq"], lb, tmid)
    y = _peer_call(x2, pq, wts["g_ffn"], wts["g_final"], wts["keys1"], wts["keys2"],
                   wts["u_tab"], wts["v_tab_t"], tb, ni)
    return (y.reshape(nb, lb, d), k.reshape(nb, lb, 2 * DA_HEADS, DA_DK),
            v.reshape(nb, lb, DA_HEADS, DA_DV), new_conv, h_last.reshape(nb, RG_WIDTH))


def _pick(n, pref):
    t = min(n, pref)
    while n % t:
        t //= 2
    return t


def kernel(x_prompt, x_sample, cache_attn_k, cache_attn_v, cache_mem_k, cache_mem_v, state_conv, state_rglru, mem_prompt, g_mix, w_in, lam_q1, lam_k1, lam_q2, lam_k2, g_subln, conv_w, conv_b, rg_wa, rg_ba, rg_wx, rg_bx, rg_lambda, w_out, g_cross, g_mem, w_cq, w_ck, w_cv, w_co, g_ffn, w_pq, pk_keys1, pk_keys2, peer_u, peer_v, g_final):
    assert g_mix.shape[0] == 1, "single-layer configuration"
    b, l, d = x_prompt.shape
    db, dl, _ = x_sample.shape
    past = cache_attn_k.shape[2]
    assert past % CHUNK == 0 and dl <= CHUNK and l % CHUNK == 0

    row = lambda a: a[0].reshape(1, -1).astype(F32)
    wts = dict(
        g_mix=row(g_mix), w_in=w_in[0].astype(BF16),
        lam=[row(lam_q1), row(lam_k1), row(lam_q2), row(lam_k2)], g_subln=row(g_subln),
        conv_w=conv_w[0], conv_b=row(conv_b),
        wbd=_block_diag_gates(rg_wa[0], rg_wx[0]),
        bax=jnp.stack([rg_ba[0], rg_bx[0]]), rg_lambda=row(rg_lambda),
        w_out=w_out[0].astype(BF16), g_cross=row(g_cross), w_cq=w_cq[0].astype(BF16),
        w_co=w_co[0].astype(BF16), g_ffn=row(g_ffn), w_pq=w_pq[0].astype(BF16),
        keys1=pk_keys1[0].astype(BF16), keys2=pk_keys2[0].astype(BF16),
        u_tab=peer_u[0].astype(BF16), v_tab_t=peer_v[0].T.astype(BF16),
        g_final=g_final.reshape(1, -1),
    )

    mk_p, mv_p = _memkv_call(mem_prompt.reshape(b * MEM_LEN, d), row(g_mem), w_ck[0].astype(BF16),
                             w_cv[0].astype(BF16), _pick(b * MEM_LEN, 256))
    mem_p = (mk_p.reshape(b, MEM_LEN, d), mv_p.reshape(b, MEM_LEN, d))
    tiles_p = (_pick(b * l, 256), _pick(l, 256), _pick(l, 256), _pick(l, 256),
               _pick(b * l, 512), 8)
    yp, kp, vp, cp, hp = _group_forward(
        x_prompt, jnp.arange(l, dtype=jnp.int32), None,
        jnp.zeros((b, CONV_W - 1, RG_WIDTH), F32), jnp.zeros((b, RG_WIDTH), F32), mem_p, wts, tiles_p)

    tiles_s = (_pick(db * dl, 256), _pick(dl, 256), dl, _pick(db * dl, 256),
               _pick(db * dl, 512), 8)
    ys, ks, vs, cs, hs = _group_forward(
        x_sample, past + jnp.arange(dl, dtype=jnp.int32),
        (cache_attn_k[0].reshape(db, past, d), cache_attn_v[0].reshape(db, past, d)),
        state_conv[0], state_rglru[0],
        (cache_mem_k[0].reshape(db, MEM_LEN, d), cache_mem_v[0].reshape(db, MEM_LEN, d)), wts, tiles_s)

    return (yp, ys, kp[None], vp[None],
            mk_p.reshape(1, b, MEM_LEN, CA_HEADS, CA_DH), mv_p.reshape(1, b, MEM_LEN, CA_HEADS, CA_DH),
            cp[None], hp[None], ks[None], vs[None], cs[None], hs[None])
```

```python
import functools
import math

import jax
import jax.numpy as jnp
from jax import lax
from jax.experimental import pallas as pl
from jax.experimental.pallas import tpu as pltpu

F32 = jnp.float32
BF16 = jnp.bfloat16

D_MODEL = 1024
CHUNK = 64
RMS_EPS = 1e-6
SUBLN_EPS = 1e-5
ROPE_THETA = 10000.0
DA_HEADS = 8
DA_DK = 64
DA_DV = 2 * DA_DK
RG_WIDTH = D_MODEL
RG_BLOCKS = 16
RG_BW = RG_WIDTH // RG_BLOCKS
RG_C = 8.0
CONV_W = 4
IN_SECTIONS = 7
IN_COLS = IN_SECTIONS * D_MODEL
MEM_LEN = 256
CA_HEADS = 4
CA_DH = D_MODEL // CA_HEADS
PK_HEADS = 8
PK_DQ = 256
PK_DHALF = PK_DQ // 2
PK_NKEYS = 128
PK_EXPERTS = PK_NKEYS * PK_NKEYS
PK_TOPK = 16
LAM_INIT = 0.8 - 0.6 * math.exp(-0.3 * 0)

LANES = 128
SUBLANES = 8
RG_GROUP = 256
NEG_BIG = -1e30

PAIR_COLS = tuple(PK_TOPK // (a + 1) for a in range(PK_TOPK))


def _cparams(semantics, vmem_mib):
    return pltpu.CompilerParams(dimension_semantics=semantics, vmem_limit_bytes=vmem_mib << 20)


def _rmsnorm(x, g, eps=RMS_EPS):
    return x * lax.rsqrt(jnp.mean(x * x, axis=-1, keepdims=True) + eps) * g


def _gelu(x):
    return 0.5 * x * (1.0 + lax.erf(x * (2.0 ** -0.5)))


def _sigmoid(x):
    return 1.0 / (1.0 + jnp.exp(-x))


def _dot(a, b):
    return jnp.dot(a, b, preferred_element_type=F32)


def _dot_nt(a, b):
    return lax.dot_general(a, b, (((1,), (1,)), ((), ())), preferred_element_type=F32)


def _inproj_kernel(x_ref, g_ref, w_ref, cos_ref, sin_ref,
                   q_ref, k_ref, v_ref, rx_ref, grg_ref, sga_ref, sgr_ref):
    n = _rmsnorm(x_ref[...], g_ref[...]).astype(BF16)
    cos = cos_ref[...]
    sin = sin_ref[...]
    lane = lax.broadcasted_iota(jnp.int32, cos.shape, 1)
    first_half = (lane & (DA_DK - 1)) < (DA_DK // 2)

    def proj(s):
        return _dot(n, w_ref[:, s * D_MODEL:(s + 1) * D_MODEL])

    def rope_cols(z, c):
        zc = z[:, c * LANES:(c + 1) * LANES]
        partner = jnp.where(first_half, pltpu.roll(zc, LANES - DA_DK // 2, 1),
                            pltpu.roll(zc, DA_DK // 2, 1))
        return zc * cos + partner * sin

    zq = proj(0)
    for c in range(D_MODEL // LANES):
        q_ref[:, c * LANES:(c + 1) * LANES] = (rope_cols(zq, c) * (DA_DK ** -0.5)).astype(BF16)
    zk = proj(1)
    for c in range(D_MODEL // LANES):
        k_ref[:, c * LANES:(c + 1) * LANES] = rope_cols(zk, c)
    v_ref[...] = proj(2)
    rx_ref[...] = proj(3)
    grg_ref[...] = _gelu(proj(4))
    sga_ref[...] = _sigmoid(proj(5))
    sgr_ref[...] = _sigmoid(proj(6))


def _inproj_call(x, g_mix, w_in, cos_t, sin_t, tm):
    t = x.shape[0]
    period = cos_t.shape[0] // tm
    row = lambda i: (i, 0)
    fixed = lambda i: (0, 0)
    f32_out = jax.ShapeDtypeStruct((t, D_MODEL), F32)
    return pl.pallas_call(
        _inproj_kernel,
        grid=(t // tm,),
        in_specs=[pl.BlockSpec((tm, D_MODEL), row),
                  pl.BlockSpec((1, D_MODEL), fixed),
                  pl.BlockSpec((D_MODEL, IN_COLS), fixed),
                  pl.BlockSpec((tm, LANES), lambda i: (i % period, 0)),
                  pl.BlockSpec((tm, LANES), lambda i: (i % period, 0))],
        out_specs=[pl.BlockSpec((tm, D_MODEL), row)] * 7,
        out_shape=[jax.ShapeDtypeStruct((t, D_MODEL), BF16)] + [f32_out] * 6,
        compiler_params=_cparams(("parallel",), 56),
        name="inproj",
    )(x, g_mix, w_in, cos_t, sin_t)


def _rglru_kernel(rx_ref, grg_ref, cst_ref, h0_ref, cw_ref, cb_ref, wbd_ref, bax_ref, lam_ref,
                  yrec_ref, hlast_ref, xbuf, hcar, *, tl):
    tb = pl.program_id(1)

    @pl.when(tb == 0)
    def _():
        xbuf[0:SUBLANES, :] = cst_ref[...]
        hcar[...] = h0_ref[...]

    x = rx_ref[...]
    xbuf[SUBLANES:SUBLANES + tl, :] = x
    cw = cw_ref[...]
    c = cb_ref[...] + cw[CONV_W - 1:CONV_W, :] * x
    for j in range(CONV_W - 1):
        back = CONV_W - 1 - j
        c = c + cw[j:j + 1, :] * xbuf[SUBLANES - back:SUBLANES - back + tl, :]
    xbuf[0:SUBLANES, :] = xbuf[tl:tl + SUBLANES, :]

    lam = lam_ref[...]
    softplus_neg = jnp.maximum(-lam, 0.0) + jnp.log1p(jnp.exp(-jnp.abs(lam)))
    cb16 = c.astype(BF16)
    a_parts, u_parts = [], []
    for g in range(RG_WIDTH // RG_GROUP):
        cols = slice(g * RG_GROUP, (g + 1) * RG_GROUP)
        z = _dot(cb16[:, cols], wbd_ref[g])
        r = _sigmoid(z[:, :RG_GROUP] + bax_ref[0:1, cols])
        i = _sigmoid(z[:, RG_GROUP:] + bax_ref[1:2, cols])
        log_a = (-RG_C) * r * softplus_neg[:, cols]
        th = jnp.tanh(log_a)
        a_parts.append(jnp.exp(log_a))
        u_parts.append(jnp.sqrt(-2.0 * th / (1.0 - th)) * i * c[:, cols])
    a = jnp.concatenate(a_parts, axis=1)
    u = jnp.concatenate(u_parts, axis=1)

    rowid = lax.broadcasted_iota(jnp.int32, a.shape, 0)
    k = 1
    while k < tl:
        keep = rowid >= k
        a_prev = jnp.where(keep, pltpu.roll(a, k, 0), 1.0)
        u_prev = jnp.where(keep, pltpu.roll(u, k, 0), 0.0)
        u = a * u_prev + u
        a = a * a_prev
        k *= 2
    h = a * hcar[...] + u
    hcar[...] = h[tl - 1:tl, :]
    yrec_ref[...] = grg_ref[...] * h

    @pl.when(tb == pl.num_programs(1) - 1)
    def _():
        hlast_ref[...] = h[tl - 1:tl, :]


def _rglru_call(rx, grg, conv_state8, h0, conv_w, conv_b, wbd, bax, lam, nb, lb, tl):
    w = RG_WIDTH
    nt = lb // tl
    fixed2 = lambda b, t: (0, 0)
    yrec, hlast = pl.pallas_call(
        functools.partial(_rglru_kernel, tl=tl),
        grid=(nb, nt),
        in_specs=[pl.BlockSpec((tl, w), lambda b, t: (b * nt + t, 0)),
                  pl.BlockSpec((tl, w), lambda b, t: (b * nt + t, 0)),
                  pl.BlockSpec((None, SUBLANES, w), lambda b, t: (b, 0, 0)),
                  pl.BlockSpec((None, 1, w), lambda b, t: (b, 0, 0)),
                  pl.BlockSpec((CONV_W, w), fixed2),
                  pl.BlockSpec((1, w), fixed2),
                  pl.BlockSpec((w // RG_GROUP, RG_GROUP, 2 * RG_GROUP), lambda b, t: (0, 0, 0)),
                  pl.BlockSpec((2, w), fixed2),
                  pl.BlockSpec((1, w), fixed2)],
        out_specs=[pl.BlockSpec((tl, w), lambda b, t: (b * nt + t, 0)),
                   pl.BlockSpec((None, 1, w), lambda b, t: (b, 0, 0))],
        out_shape=[jax.ShapeDtypeStruct((nb * lb, w), F32),
                   jax.ShapeDtypeStruct((nb, 1, w), F32)],
        scratch_shapes=[pltpu.VMEM((tl + SUBLANES, w), F32), pltpu.VMEM((1, w), F32)],
        compiler_params=_cparams(("parallel", "arbitrary"), 40),
        name="rglru",
    )(rx, grg, conv_state8, h0, conv_w, conv_b, wbd, bax, lam)
    return yrec, hlast


def _stream_queries(q):
    lane = lax.broadcasted_iota(jnp.int32, q.shape, 1)
    zero = jnp.zeros_like(q)
    return jnp.concatenate([jnp.where(lane < DA_DK, q, zero), jnp.where(lane >= DA_DK, q, zero)], axis=0)


def _attn_init(m_s, l_s, acc_s):
    m_s[...] = jnp.full(m_s.shape, NEG_BIG, F32)
    l_s[...] = jnp.zeros(l_s.shape, F32)
    acc_s[...] = jnp.zeros(acc_s.shape, F32)


def _attn_update(s_t, v_t, m_s, l_s, acc_s):
    m_old = m_s[...]
    m_new = jnp.maximum(m_old, jnp.max(s_t, axis=0, keepdims=True))
    alpha = jnp.exp(m_old - m_new)
    p = jnp.exp(s_t - m_new)
    l_s[...] = alpha * l_s[...] + jnp.sum(p, axis=0, keepdims=True)
    acc_s[...] = alpha * acc_s[...] + _dot(v_t, p.astype(BF16))
    m_s[...] = m_new


def _chunk_mask(tk, tq):
    kchunk = lax.broadcasted_iota(jnp.int32, (tk, 2 * tq), 0) // CHUNK
    col = lax.broadcasted_iota(jnp.int32, (tk, 2 * tq), 1)
    qchunk = jnp.where(col >= tq, col - tq, col) // CHUNK
    return kchunk <= qchunk


def _attn_finish(lam_refs, gsub, l_s, acc_s, tq):
    lq1, lk1, lq2, lk2 = lam_refs
    lam = (jnp.exp(jnp.sum(lq1[...] * lk1[...], axis=1, keepdims=True))
           - jnp.exp(jnp.sum(lq2[...] * lk2[...], axis=1, keepdims=True)) + LAM_INIT)
    o = acc_s[...] / l_s[...]
    if tq % LANES == 0:
        o_t = o[:, :tq] - lam * o[:, tq:]
    else:
        o_t = o - lam * pltpu.roll(o, tq, 1)
    o_t = o_t * lax.rsqrt(jnp.mean(o_t * o_t, axis=0, keepdims=True) + SUBLN_EPS)
    return o_t.T[:tq, :] * gsub * (1.0 - LAM_INIT)


def _attn_prompt_kernel(lq1, lk1, lq2, lk2, gsub_ref, q_ref, k_ref, v_ref, o_ref,
                        kb_s, vt_s, m_s, l_s, acc_s, *, tq, nblk):
    qi = pl.program_id(2)

    @pl.when(qi == 0)
    def _():
        for j in range(nblk):
            rows = slice(j * tq, (j + 1) * tq)
            kb_s[rows, :] = k_ref[rows, :].astype(BF16)
            vt_s[:, rows] = v_ref[rows, :].T.astype(BF16)

    qcat = _stream_queries(q_ref[...])
    _attn_init(m_s, l_s, acc_s)

    def body(j, carry):
        start = pl.multiple_of(j * tq, tq)
        s_t = _dot_nt(kb_s[pl.ds(start, tq), :], qcat)
        _attn_update(s_t, vt_s[:, pl.ds(start, tq)], m_s, l_s, acc_s)
        return carry
    lax.fori_loop(0, qi, body, 0)

    start = pl.multiple_of(qi * tq, tq)
    s_t = jnp.where(_chunk_mask(tq, tq), _dot_nt(kb_s[pl.ds(start, tq), :], qcat), NEG_BIG)
    _attn_update(s_t, vt_s[:, pl.ds(start, tq)], m_s, l_s, acc_s)
    o_ref[...] = _attn_finish((lq1, lk1, lq2, lk2), gsub_ref[...], l_s, acc_s, tq)


def _attn_prompt_call(q, k, v, lam_params, g_subln, nb, lb, tq):
    nq = lb // tq
    vec = lambda b, h, i: (0, 0)
    in_specs = [pl.BlockSpec((1, DA_DK), vec)] * 4 + [
        pl.BlockSpec((1, DA_DV), vec),
        pl.BlockSpec((tq, DA_DV), lambda b, h, i: (b * nq + i, h)),
        pl.BlockSpec((None, lb, DA_DV), lambda b, h, i: (b, 0, h)),
        pl.BlockSpec((None, lb, DA_DV), lambda b, h, i: (b, 0, h))]
    return pl.pallas_call(
        functools.partial(_attn_prompt_kernel, tq=tq, nblk=nq),
        grid=(nb, DA_HEADS, nq),
        in_specs=in_specs,
        out_specs=pl.BlockSpec((tq, DA_DV), lambda b, h, i: (b * nq + i, h)),
        out_shape=jax.ShapeDtypeStruct((nb * lb, D_MODEL), F32),
        scratch_shapes=[pltpu.VMEM((lb, DA_DV), BF16), pltpu.VMEM((DA_DV, lb), BF16),
                        pltpu.VMEM((1, 2 * tq), F32), pltpu.VMEM((1, 2 * tq), F32),
                        pltpu.VMEM((DA_DV, 2 * tq), F32)],
        compiler_params=_cparams(("parallel", "parallel", "arbitrary"), 40),
        name="diff_attn_prompt",
    )(*lam_params, g_subln, q, k, v)


def _attn_sample_kernel(lq1, lk1, lq2, lk2, gsub_ref, q_ref, k_ref, v_ref, kp_ref, vp_ref, o_ref,
                        m_s, l_s, acc_s, *, dl):
    j = pl.program_id(1)

    @pl.when(j == 0)
    def _():
        _attn_init(m_s, l_s, acc_s)

    for h in range(DA_HEADS):
        cols = slice(h * DA_DV, (h + 1) * DA_DV)
        qcat = _stream_queries(q_ref[:, cols])
        s_t = _dot_nt(kp_ref[:, cols], qcat)
        v_t = vp_ref[:, cols].astype(F32).T.astype(BF16)
        _attn_update(s_t, v_t, m_s.at[h], l_s.at[h], acc_s.at[h])

    @pl.when(j == pl.num_programs(1) - 1)
    def _():
        mask = _chunk_mask(dl, dl)
        for h in range(DA_HEADS):
            cols = slice(h * DA_DV, (h + 1) * DA_DV)
            qcat = _stream_queries(q_ref[:, cols])
            s_t = jnp.where(mask, _dot_nt(k_ref[:, cols].astype(BF16), qcat), NEG_BIG)
            _attn_update(s_t, v_ref[:, cols].T.astype(BF16), m_s.at[h], l_s.at[h], acc_s.at[h])
            o_ref[:, cols] = _attn_finish((lq1, lk1, lq2, lk2), gsub_ref[...], l_s.at[h], acc_s.at[h], dl)


def _attn_sample_call(q, k, v, k_past, v_past, lam_params, g_subln, nb, dl, tkp):
    p = k_past.shape[1]
    vec = lambda b, j: (0, 0)
    new = pl.BlockSpec((dl, D_MODEL), lambda b, j: (b, 0))
    past = pl.BlockSpec((None, tkp, D_MODEL), lambda b, j: (b, j, 0))
    return pl.pallas_call(
        functools.partial(_attn_sample_kernel, dl=dl),
        grid=(nb, p // tkp),
        in_specs=[pl.BlockSpec((1, DA_DK), vec)] * 4 + [pl.BlockSpec((1, DA_DV), vec),
                                                         new, new, new, past, past],
        out_specs=new,
        out_shape=jax.ShapeDtypeStruct((nb * dl, D_MODEL), F32),
        scratch_shapes=[pltpu.VMEM((DA_HEADS, 1, 2 * dl), F32), pltpu.VMEM((DA_HEADS, 1, 2 * dl), F32),
                        pltpu.VMEM((DA_HEADS, DA_DV, 2 * dl), F32)],
        compiler_params=_cparams(("parallel", "arbitrary"), 40),
        name="diff_attn_sample",
    )(*lam_params, g_subln, q, k, v, k_past, v_past)


def _memkv_kernel(m_ref, g_ref, wk_ref, wv_ref, k_ref, v_ref):
    mn = _rmsnorm(m_ref[...], g_ref[...]).astype(BF16)
    k_ref[...] = _dot(mn, wk_ref[...])
    v_ref[...] = _dot(mn, wv_ref[...])


def _memkv_call(mem, g_mem, w_ck, w_cv, tm):
    t = mem.shape[0]
    row = lambda i: (i, 0)
    fixed = lambda i: (0, 0)
    return pl.pallas_call(
        _memkv_kernel,
        grid=(t // tm,),
        in_specs=[pl.BlockSpec((tm, D_MODEL), row), pl.BlockSpec((1, D_MODEL), fixed),
                  pl.BlockSpec((D_MODEL, D_MODEL), fixed), pl.BlockSpec((D_MODEL, D_MODEL), fixed)],
        out_specs=[pl.BlockSpec((tm, D_MODEL), row)] * 2,
        out_shape=[jax.ShapeDtypeStruct((t, D_MODEL), F32)] * 2,
        compiler_params=_cparams(("parallel",), 32),
        name="memory_kv",
    )(mem, g_mem, w_ck, w_cv)


def _mid_kernel(x_ref, ya_ref, yr_ref, sga_ref, sgr_ref, wo_ref, gc_ref, wcq_ref, mk_ref, mv_ref,
                wco_ref, gf_ref, wpq_ref, x2_ref, pq_ref, *, groups, rows):
    merged = (sga_ref[...] * ya_ref[...] + sgr_ref[...] * yr_ref[...]).astype(BF16)
    x1 = x_ref[...] + _dot(merged, wo_ref[...])
    qc = _dot(_rmsnorm(x1, gc_ref[...]).astype(BF16), wcq_ref[...]).astype(BF16)
    o_rows = []
    for g in range(groups):
        mk = mk_ref[g].astype(BF16)
        mv = mv_ref[g].astype(BF16)
        o_heads = []
        for h in range(CA_HEADS):
            cols = slice(h * CA_DH, (h + 1) * CA_DH)
            s = _dot_nt(qc[g * rows:(g + 1) * rows, cols], mk[:, cols]) * (CA_DH ** -0.5)
            p = jnp.exp(s - jnp.max(s, axis=1, keepdims=True))
            p = p / jnp.sum(p, axis=1, keepdims=True)
            o_heads.append(_dot(p.astype(BF16), mv[:, cols]))
        o_rows.append(jnp.concatenate(o_heads, axis=1))
    o = o_rows[0] if groups == 1 else jnp.concatenate(o_rows, axis=0)
    x2 = x1 + _dot(o.astype(BF16), wco_ref[...])
    x2_ref[...] = x2
    pq_ref[...] = _dot(_rmsnorm(x2, gf_ref[...]).astype(BF16), wpq_ref[...]).astype(BF16)


def _mid_call(x, y_att, y_rec, sga, sgr, w_out, g_cross, w_cq, mem_k, mem_v, w_co, g_ffn, w_pq,
              lb, tm):
    t = x.shape[0]
    rows = min(tm, lb)
    groups = tm // rows
    per_mem = lb // rows
    row = lambda i: (i, 0)
    fixed = lambda i: (0, 0)
    mem_spec = pl.BlockSpec((groups, MEM_LEN, D_MODEL), lambda i: (i // per_mem, 0, 0))
    act = pl.BlockSpec((tm, D_MODEL), row)
    wsq = pl.BlockSpec((D_MODEL, D_MODEL), fixed)
    gvec = pl.BlockSpec((1, D_MODEL), fixed)
    return pl.pallas_call(
        functools.partial(_mid_kernel, groups=groups, rows=rows),
        grid=(t // tm,),
        in_specs=[act, act, act, act, act, wsq, gvec, wsq, mem_spec, mem_spec, wsq, gvec,
                  pl.BlockSpec((D_MODEL, PK_HEADS * PK_DQ), fixed)],
        out_specs=[act, pl.BlockSpec((tm, PK_HEADS * PK_DQ), row)],
        out_shape=[jax.ShapeDtypeStruct((t, D_MODEL), F32),
                   jax.ShapeDtypeStruct((t, PK_HEADS * PK_DQ), BF16)],
        compiler_params=_cparams(("parallel",), 56),
        name="merge_cross_attn",
    )(x, y_att, y_rec, sga, sgr, w_out, g_cross, w_cq, mem_k, mem_v, w_co, g_ffn, w_pq)


def _extract_top(s_ref, order, t_ref, rank_ref):
    rank_ref[...] = jnp.full(rank_ref.shape, float(PK_TOPK), F32)

    def body(r, carry):
        s = s_ref[...]
        m = jnp.max(s, axis=0, keepdims=True)
        first = jnp.min(jnp.where(s == m, order, 1e9), axis=0, keepdims=True)
        sel = order == first
        t_ref[pl.ds(r, 1), :] = m
        s_ref[...] = jnp.where(sel, -jnp.inf, s)
        rank_ref[...] = jnp.where(sel, r.astype(F32), rank_ref[...])
        return carry
    lax.fori_loop(0, PK_TOPK, body, 0)


def _route_head(pq_ref, k1_ref, k2_ref, h, n1_s, p_s, r2_s, q_s, zinv_s,
                s_s, rank1_s, rank2_s, cand_s, rankc_s, t1_s, t2_s, ts_s, na_s):
    tb = pq_ref.shape[0]
    key_id = lax.broadcasted_iota(jnp.int32, (PK_NKEYS, tb), 0).astype(F32)
    s1 = _dot_nt(k1_ref[h], pq_ref[:, (2 * h) * PK_DHALF:(2 * h + 1) * PK_DHALF])
    groups = [slice(c * LANES, (c + 1) * LANES) for c in range(tb // LANES)]
    p1 = jnp.exp(s1 - jnp.max(s1, axis=0, keepdims=True))
    for c, lanes in enumerate(groups):
        p_s[h, c] = p1[:, lanes]
    s_s[...] = s1
    _extract_top(s_s, key_id, t1_s, rank1_s)
    s2 = _dot_nt(k2_ref[h], pq_ref[:, (2 * h + 1) * PK_DHALF:(2 * h + 2) * PK_DHALF])
    q2 = jnp.exp(s2 - jnp.max(s2, axis=0, keepdims=True))
    for c, lanes in enumerate(groups):
        q_s[h, c] = q2[:, lanes]
    s_s[...] = s2
    _extract_top(s_s, key_id, t2_s, rank2_s)
    for c, lanes in enumerate(groups):
        r2_s[h, c] = rank2_s[:, lanes]
    t1 = t1_s[...]
    t2 = t2_s[...]

    row16 = lax.broadcasted_iota(jnp.int32, (PK_TOPK, tb), 0).astype(F32)
    row8 = lax.broadcasted_iota(jnp.int32, (SUBLANES, tb), 0).astype(F32)
    cand_s[0:PK_TOPK, :] = t1[0:1] + t2
    order = [row16]
    for a in range(1, SUBLANES):
        lo = PK_TOPK + (a - 1) * SUBLANES
        cand_s[lo:lo + SUBLANES, :] = jnp.where(row8 < PAIR_COLS[a], t1[a:a + 1] + t2[:SUBLANES], -jnp.inf)
        order.append(row8 + float(a * PK_TOPK))
    last = PK_TOPK + (SUBLANES - 1) * SUBLANES
    cand_s[last:, :] = t1[SUBLANES:] + t2[0:1]
    order.append((row8 + float(SUBLANES)) * float(PK_TOPK))
    _extract_top(cand_s, jnp.concatenate(order, axis=0), ts_s, rankc_s)
    picked = jnp.where(rankc_s[...] < PK_TOPK, 1.0, 0.0)
    na_s[0:1, :] = jnp.sum(picked[0:PK_TOPK], axis=0, keepdims=True)
    for a in range(1, SUBLANES):
        lo = PK_TOPK + (a - 1) * SUBLANES
        na_s[a:a + 1, :] = jnp.sum(picked[lo:lo + SUBLANES], axis=0, keepdims=True)
    na_s[SUBLANES:, :] = picked[last:]
    ts = ts_s[...]
    zinv_s[h] = 1.0 / jnp.sum(jnp.exp(ts - ts[0:1]), axis=0, keepdims=True)

    for c, lanes in enumerate(groups):
        rank1 = rank1_s[:, lanes]
        n1 = jnp.zeros((PK_NKEYS, LANES), F32)
        for a in range(PK_TOPK):
            n1 = jnp.where(rank1 == float(a), na_s[a:a + 1, lanes], n1)
        n1_s[h, c] = n1


def _peer_kernel(x2_ref, pq_ref, gf_ref, gfin_ref, k1_ref, k2_ref, u_ref, vt_ref, y_ref,
                 xft_s, n1_s, p_s, r2_s, q_s, zinv_s, s_s, rank1_s, rank2_s, cand_s, rankc_s,
                 t1_s, t2_s, ts_s, na_s, act0_s, act1_s, wa0_s, wa1_s, acc_s, *, ni):
    s = pl.program_id(1)
    n_blocks = pl.num_programs(1) - 2
    tb = xft_s.shape[1]
    eb = ni * PK_NKEYS

    @pl.when(s == 0)
    def _():
        xft_s[...] = _rmsnorm(x2_ref[...], gf_ref[...]).T.astype(BF16)
        acc_s[...] = jnp.zeros(acc_s.shape, F32)
        act1_s[...] = jnp.zeros(act1_s.shape, F32)
        wa0_s[...] = jnp.zeros(wa0_s.shape, BF16)
        for h in range(PK_HEADS):
            _route_head(pq_ref, k1_ref, k2_ref, h, n1_s, p_s, r2_s, q_s, zinv_s,
                        s_s, rank1_s, rank2_s, cand_s, rankc_s, t1_s, t2_s, ts_s, na_s)

    def stages(act_new, act_cur, wa_new, wa_done):
        blk = s - 1
        live = jnp.where((blk >= 0) & (blk < n_blocks), 1.0, 0.0)
        base = pl.multiple_of(jnp.clip(blk, 0, n_blocks - 1) * ni, ni)
        groups = [slice(c * LANES, (c + 1) * LANES) for c in range(tb // LANES)]
        n1_rows = [[n1_s[h, c, pl.ds(base, ni), :] for c in range(len(groups))] for h in range(PK_HEADS)]
        gate_rows = [[p_s[h, c, pl.ds(base, ni), :] * (zinv_s[h, :, lanes] * live)
                      for c, lanes in enumerate(groups)] for h in range(PK_HEADS)]
        n_chunks = ni // 2
        for j in range(n_chunks):
            er = slice(j * (eb // n_chunks), (j + 1) * (eb // n_chunks))
            dr = slice(j * (D_MODEL // n_chunks), (j + 1) * (D_MODEL // n_chunks))
            act_new[er, :] = _dot(u_ref[er, :], xft_s[...])
            acc_s[dr, :] += _dot(vt_ref[dr, :], wa_done[...])
            for ii in range(2 * j, 2 * j + 2):
                rows = slice(ii * PK_NKEYS, (ii + 1) * PK_NKEYS)
                for c, lanes in enumerate(groups):
                    w = jnp.zeros((PK_NKEYS, LANES), F32)
                    for h in range(PK_HEADS):
                        picked = r2_s[h, c] < n1_rows[h][c][ii:ii + 1, :]
                        w = w + jnp.where(picked, q_s[h, c], 0.0) * gate_rows[h][c][ii:ii + 1, :]
                    wa_new[rows, lanes] = (w * _gelu(act_cur[rows, lanes])).astype(BF16)

    @pl.when(s % 2 == 0)
    def _():
        stages(act0_s, act1_s, wa1_s, wa0_s)

    @pl.when(s % 2 == 1)
    def _():
        stages(act1_s, act0_s, wa0_s, wa1_s)

    @pl.when(s == pl.num_programs(1) - 1)
    def _():
        y_ref[...] = _rmsnorm(x2_ref[...] + acc_s[...].T, gfin_ref[...])


def _peer_call(x2, pq, g_ffn, g_final, keys1, keys2, u_tab, v_tab_t, tb, ni):
    t = x2.shape[0]
    eb = ni * PK_NKEYS
    ne = PK_EXPERTS // eb
    n_cand = PK_TOPK + SUBLANES * SUBLANES
    fixed = lambda i, e: (0, 0)
    keys_spec = pl.BlockSpec((PK_HEADS, PK_NKEYS, PK_DHALF), lambda i, e: (0, 0, 0))
    route = pltpu.VMEM((PK_HEADS, tb // LANES, PK_NKEYS, LANES), F32)
    keys = pltpu.VMEM((PK_NKEYS, tb), F32)
    cand = pltpu.VMEM((n_cand, tb), F32)
    top = pltpu.VMEM((PK_TOPK, tb), F32)
    return pl.pallas_call(
        functools.partial(_peer_kernel, ni=ni),
        grid=(t // tb, ne + 2),
        in_specs=[pl.BlockSpec((tb, D_MODEL), lambda i, e: (i, 0)),
                  pl.BlockSpec((tb, PK_HEADS * PK_DQ), lambda i, e: (i, 0)),
                  pl.BlockSpec((1, D_MODEL), fixed), pl.BlockSpec((1, D_MODEL), fixed),
                  keys_spec, keys_spec,
                  pl.BlockSpec((eb, D_MODEL), lambda i, e: (jnp.minimum(e, ne - 1), 0)),
                  pl.BlockSpec((D_MODEL, eb), lambda i, e: (0, jnp.clip(e - 2, 0, ne - 1)))],
        out_specs=pl.BlockSpec((tb, D_MODEL), lambda i, e: (i, 0)),
        out_shape=jax.ShapeDtypeStruct((t, D_MODEL), F32),
        scratch_shapes=[pltpu.VMEM((D_MODEL, tb), BF16), route, route, route, route,
                        pltpu.VMEM((PK_HEADS, 1, tb), F32), keys, keys, keys, cand, cand,
                        top, top, top, top,
                        pltpu.VMEM((eb, tb), F32), pltpu.VMEM((eb, tb), F32),
                        pltpu.VMEM((eb, tb), BF16), pltpu.VMEM((eb, tb), BF16),
                        pltpu.VMEM((D_MODEL, tb), F32)],
        compiler_params=_cparams(("parallel", "arbitrary"), 56),
        name="peer",
    )(x2, pq, g_ffn, g_final, keys1, keys2, u_tab, v_tab_t)


def _rope_tables(pos, rows):
    half = DA_DK // 2
    inv = jnp.exp(-math.log(ROPE_THETA) * jnp.arange(half, dtype=F32) * (2.0 / DA_DK))
    ang = pos.astype(F32)[:, None] * inv[None, :]
    cos = jnp.tile(jnp.cos(ang), (1, LANES // half))
    sin = jnp.tile(jnp.concatenate([-jnp.sin(ang), jnp.sin(ang)], axis=1), (1, LANES // DA_DK))
    reps = max(1, rows // pos.shape[0])
    return jnp.tile(cos, (reps, 1)), jnp.tile(sin, (reps, 1))


def _block_diag_gates(wa, wx):
    per = RG_GROUP // RG_BW
    def bd(w):
        w = w.reshape(RG_WIDTH // RG_GROUP, per, RG_BW, RG_BW)
        eye = jnp.eye(per, dtype=w.dtype)
        return jnp.einsum("gmcd,mn->gmcnd", w, eye).reshape(RG_WIDTH // RG_GROUP, RG_GROUP, RG_GROUP)
    return jnp.concatenate([bd(wa), bd(wx)], axis=2).astype(BF16)


def _group_forward(x, pos, past_kv, conv_state, h0, mem_kv, wts, tiles):
    nb, lb, d = x.shape
    t = nb * lb
    tm, tl, tq, tmid, tb, ni = tiles
    xf = x.reshape(t, d)
    cos_t, sin_t = _rope_tables(pos, tm)
    q, k, v, rx, grg, sga, sgr = _inproj_call(xf, wts["g_mix"], wts["w_in"], cos_t, sin_t, tm)

    conv8 = jnp.concatenate(
        [jnp.zeros((nb, SUBLANES - (CONV_W - 1), RG_WIDTH), F32), conv_state.astype(F32)], axis=1)
    y_rec, h_last = _rglru_call(rx, grg, conv8, h0.reshape(nb, 1, RG_WIDTH), wts["conv_w"],
                                wts["conv_b"], wts["wbd"], wts["bax"], wts["rg_lambda"], nb, lb, tl)
    new_conv = jnp.concatenate([conv_state.astype(F32), rx.reshape(nb, lb, RG_WIDTH)],
                               axis=1)[:, lb:]

    if past_kv is None:
        y_att = _attn_prompt_call(q, k.reshape(nb, lb, d), v.reshape(nb, lb, d), wts["lam"],
                                  wts["g_subln"], nb, lb, tq)
    else:
        y_att = _attn_sample_call(q, k, v, past_kv[0], past_kv[1], wts["lam"], wts["g_subln"], nb, lb,
                                  _pick(past_kv[0].shape[1], 1024))

    x2, pq = _mid_call(xf, y_att, y_rec, sga, sgr, wts["w_out"], wts["g_cross"], wts["w_cq"],
                       mem_kv[0], mem_kv[1], wts["w_co"], wts["g_ffn"], wts["w_pq"], lb, tmid)
    y = _peer_call(x2, pq, wts["g_ffn"], wts["g_final"], wts["keys1"], wts["keys2"],
                   wts["u_tab"], wts["v_tab_t"], tb, ni)
    return (y.reshape(nb, lb, d), k.reshape(nb, lb, 2 * DA_HEADS, DA_DK),
            v.reshape(nb, lb, DA_HEADS, DA_DV), new_conv, h_last.reshape(nb, RG_WIDTH))


def _pick(n, pref):
    t = min(n, pref)
    while n % t:
        t //= 2
    return t


def kernel(x_prompt, x_sample, cache_attn_k, cache_attn_v, cache_mem_k, cache_mem_v, state_conv, state_rglru, mem_prompt, g_mix, w_in, lam_q1, lam_k1, lam_q2, lam_k2, g_subln, conv_w, conv_b, rg_wa, rg_ba, rg_wx, rg_bx, rg_lambda, w_out, g_cross, g_mem, w_cq, w_ck, w_cv, w_co, g_ffn, w_pq, pk_keys1, pk_keys2, peer_u, peer_v, g_final):
    assert g_mix.shape[0] == 1, "single-layer configuration"
    b, l, d = x_prompt.shape
    db, dl, _ = x_sample.shape
    past = cache_attn_k.shape[2]
    assert past % CHUNK == 0 and dl <= CHUNK and l % CHUNK == 0

    row = lambda a: a[0].reshape(1, -1).astype(F32)
    wts = dict(
        g_mix=row(g_mix), w_in=w_in[0].astype(BF16),
        lam=[row(lam_q1), row(lam_k1), row(lam_q2), row(lam_k2)], g_subln=row(g_subln),
        conv_w=conv_w[0], conv_b=row(conv_b),
        wbd=_block_diag_gates(rg_wa[0], rg_wx[0]),
        bax=jnp.stack([rg_ba[0], rg_bx[0]]), rg_lambda=row(rg_lambda),
        w_out=w_out[0].astype(BF16), g_cross=row(g_cross), w_cq=w_cq[0].astype(BF16),
        w_co=w_co[0].astype(BF16), g_ffn=row(g_ffn), w_pq=w_pq[0].astype(BF16),
        keys1=pk_keys1[0].astype(BF16), keys2=pk_keys2[0].astype(BF16),
        u_tab=peer_u[0].astype(BF16), v_tab_t=peer_v[0].T.astype(BF16),
        g_final=g_final.reshape(1, -1),
    )

    mk_p, mv_p = _memkv_call(mem_prompt.reshape(b * MEM_LEN, d), row(g_mem), w_ck[0].astype(BF16),
                             w_cv[0].astype(BF16), _pick(b * MEM_LEN, 256))
    mem_p = (mk_p.reshape(b, MEM_LEN, d), mv_p.reshape(b, MEM_LEN, d))
    tiles_p = (_pick(b * l, 256), _pick(l, 256), _pick(l, 512), _pick(l, 256),
               _pick(b * l, 512), 8)
    yp, kp, vp, cp, hp = _group_forward(
        x_prompt, jnp.arange(l, dtype=jnp.int32), None,
        jnp.zeros((b, CONV_W - 1, RG_WIDTH), F32), jnp.zeros((b, RG_WIDTH), F32), mem_p, wts, tiles_p)

    tiles_s = (_pick(db * dl, 256), _pick(dl, 256), dl, _pick(db * dl, 256),
               _pick(db * dl, 512), 8)
    ys, ks, vs, cs, hs = _group_forward(
        x_sample, past + jnp.arange(dl, dtype=jnp.int32),
        (cache_attn_k[0].astype(BF16).reshape(db, past, d),
         cache_attn_v[0].astype(BF16).reshape(db, past, d)),
        state_conv[0], state_rglru[0],
        (cache_mem_k[0].reshape(db, MEM_LEN, d), cache_mem_v[0].reshape(db, MEM_LEN, d)), wts, tiles_s)

    return (yp, ys, kp[None], vp[None],
            mk_p.reshape(1, b, MEM_LEN, CA_HEADS, CA_DH), mv_p.reshape(1, b, MEM_LEN, CA_HEADS, CA_DH),
            cp[None], hp[None], ks[None], vs[None], cs[None], hs[None])
```

```python
import functools
import math

import jax
import jax.numpy as jnp
from jax import lax
from jax.experimental import pallas as pl
from jax.experimental.pallas import tpu as pltpu

F32 = jnp.float32
BF16 = jnp.bfloat16

D_MODEL = 1024
CHUNK = 64
RMS_EPS = 1e-6
SUBLN_EPS = 1e-5
ROPE_THETA = 10000.0
DA_HEADS = 8
DA_DK = 64
DA_DV = 2 * DA_DK
RG_WIDTH = D_MODEL
RG_BLOCKS = 16
RG_BW = RG_WIDTH // RG_BLOCKS
RG_C = 8.0
CONV_W = 4
IN_SECTIONS = 7
IN_COLS = IN_SECTIONS * D_MODEL
MEM_LEN = 256
CA_HEADS = 4
CA_DH = D_MODEL // CA_HEADS
PK_HEADS = 8
PK_DQ = 256
PK_DHALF = PK_DQ // 2
PK_NKEYS = 128
PK_EXPERTS = PK_NKEYS * PK_NKEYS
PK_TOPK = 16
LAM_INIT = 0.8 - 0.6 * math.exp(-0.3 * 0)
PEER_BLOCK_EXPERTS = 8 * PK_NKEYS

LANES = 128
SUBLANES = 8
RG_GROUP = 256
NEG_BIG = -1e30

PAIR_COLS = tuple(PK_TOPK // (a + 1) for a in range(PK_TOPK))


def _cparams(semantics, vmem_mib):
    return pltpu.CompilerParams(dimension_semantics=semantics, vmem_limit_bytes=vmem_mib << 20)


def _rmsnorm(x, g, eps=RMS_EPS):
    return x * lax.rsqrt(jnp.mean(x * x, axis=-1, keepdims=True) + eps) * g


def _gelu(x):
    return 0.5 * x * (1.0 + lax.erf(x * (2.0 ** -0.5)))


def _sigmoid(x):
    return 1.0 / (1.0 + jnp.exp(-x))


def _dot(a, b):
    return jnp.dot(a, b, preferred_element_type=F32)


def _dot_nt(a, b):
    return lax.dot_general(a, b, (((1,), (1,)), ((), ())), preferred_element_type=F32)


def _inproj_kernel(x_ref, g_ref, w_ref, cos_ref, sin_ref,
                   q_ref, k_ref, v_ref, rx_ref, grg_ref, sga_ref, sgr_ref):
    n = _rmsnorm(x_ref[...], g_ref[...]).astype(BF16)
    cos = cos_ref[...]
    sin = sin_ref[...]
    lane = lax.broadcasted_iota(jnp.int32, cos.shape, 1)
    first_half = (lane & (DA_DK - 1)) < (DA_DK // 2)

    def proj(s):
        return _dot(n, w_ref[:, s * D_MODEL:(s + 1) * D_MODEL])

    def rope_cols(z, c):
        zc = z[:, c * LANES:(c + 1) * LANES]
        partner = jnp.where(first_half, pltpu.roll(zc, LANES - DA_DK // 2, 1),
                            pltpu.roll(zc, DA_DK // 2, 1))
        return zc * cos + partner * sin

    zq = proj(0)
    for c in range(D_MODEL // LANES):
        q_ref[:, c * LANES:(c + 1) * LANES] = (rope_cols(zq, c) * (DA_DK ** -0.5)).astype(BF16)
    zk = proj(1)
    for c in range(D_MODEL // LANES):
        k_ref[:, c * LANES:(c + 1) * LANES] = rope_cols(zk, c)
    v_ref[...] = proj(2)
    rx_ref[...] = proj(3)
    grg_ref[...] = _gelu(proj(4))
    sga_ref[...] = _sigmoid(proj(5))
    sgr_ref[...] = _sigmoid(proj(6))


def _inproj_call(x, g_mix, w_in, cos_t, sin_t, tm):
    t = x.shape[0]
    period = cos_t.shape[0] // tm
    row = lambda i: (i, 0)
    fixed = lambda i: (0, 0)
    f32_out = jax.ShapeDtypeStruct((t, D_MODEL), F32)
    return pl.pallas_call(
        _inproj_kernel,
        grid=(t // tm,),
        in_specs=[pl.BlockSpec((tm, D_MODEL), row),
                  pl.BlockSpec((1, D_MODEL), fixed),
                  pl.BlockSpec((D_MODEL, IN_COLS), fixed),
                  pl.BlockSpec((tm, LANES), lambda i: (i % period, 0)),
                  pl.BlockSpec((tm, LANES), lambda i: (i % period, 0))],
        out_specs=[pl.BlockSpec((tm, D_MODEL), row)] * 7,
        out_shape=[jax.ShapeDtypeStruct((t, D_MODEL), BF16)] + [f32_out] * 6,
        compiler_params=_cparams(("parallel",), 56),
        name="inproj",
    )(x, g_mix, w_in, cos_t, sin_t)


def _rglru_kernel(rx_ref, grg_ref, cst_ref, h0_ref, cw_ref, cb_ref, wbd_ref, bax_ref, lam_ref,
                  yrec_ref, hlast_ref, xbuf, hcar, *, tl):
    tb = pl.program_id(1)

    @pl.when(tb == 0)
    def _():
        xbuf[0:SUBLANES, :] = cst_ref[...]
        hcar[...] = h0_ref[...]

    x = rx_ref[...]
    xbuf[SUBLANES:SUBLANES + tl, :] = x
    cw = cw_ref[...]
    c = cb_ref[...] + cw[CONV_W - 1:CONV_W, :] * x
    for j in range(CONV_W - 1):
        back = CONV_W - 1 - j
        c = c + cw[j:j + 1, :] * xbuf[SUBLANES - back:SUBLANES - back + tl, :]
    xbuf[0:SUBLANES, :] = xbuf[tl:tl + SUBLANES, :]

    lam = lam_ref[...]
    softplus_neg = jnp.maximum(-lam, 0.0) + jnp.log1p(jnp.exp(-jnp.abs(lam)))
    cb16 = c.astype(BF16)
    a_parts, u_parts = [], []
    for g in range(RG_WIDTH // RG_GROUP):
        cols = slice(g * RG_GROUP, (g + 1) * RG_GROUP)
        z = _dot(cb16[:, cols], wbd_ref[g])
        r = _sigmoid(z[:, :RG_GROUP] + bax_ref[0:1, cols])
        i = _sigmoid(z[:, RG_GROUP:] + bax_ref[1:2, cols])
        log_a = (-RG_C) * r * softplus_neg[:, cols]
        th = jnp.tanh(log_a)
        a_parts.append(jnp.exp(log_a))
        u_parts.append(jnp.sqrt(-2.0 * th / (1.0 - th)) * i * c[:, cols])
    a = jnp.concatenate(a_parts, axis=1)
    u = jnp.concatenate(u_parts, axis=1)

    rowid = lax.broadcasted_iota(jnp.int32, a.shape, 0)
    k = 1
    while k < tl:
        keep = rowid >= k
        a_prev = jnp.where(keep, pltpu.roll(a, k, 0), 1.0)
        u_prev = jnp.where(keep, pltpu.roll(u, k, 0), 0.0)
        u = a * u_prev + u
        a = a * a_prev
        k *= 2
    h = a * hcar[...] + u
    hcar[...] = h[tl - 1:tl, :]
    yrec_ref[...] = grg_ref[...] * h

    @pl.when(tb == pl.num_programs(1) - 1)
    def _():
        hlast_ref[...] = h[tl - 1:tl, :]


def _rglru_call(rx, grg, conv_state8, h0, conv_w, conv_b, wbd, bax, lam, nb, lb, tl):
    w = RG_WIDTH
    nt = lb // tl
    fixed2 = lambda b, t: (0, 0)
    yrec, hlast = pl.pallas_call(
        functools.partial(_rglru_kernel, tl=tl),
        grid=(nb, nt),
        in_specs=[pl.BlockSpec((tl, w), lambda b, t: (b * nt + t, 0)),
                  pl.BlockSpec((tl, w), lambda b, t: (b * nt + t, 0)),
                  pl.BlockSpec((None, SUBLANES, w), lambda b, t: (b, 0, 0)),
                  pl.BlockSpec((None, 1, w), lambda b, t: (b, 0, 0)),
                  pl.BlockSpec((CONV_W, w), fixed2),
                  pl.BlockSpec((1, w), fixed2),
                  pl.BlockSpec((w // RG_GROUP, RG_GROUP, 2 * RG_GROUP), lambda b, t: (0, 0, 0)),
                  pl.BlockSpec((2, w), fixed2),
                  pl.BlockSpec((1, w), fixed2)],
        out_specs=[pl.BlockSpec((tl, w), lambda b, t: (b * nt + t, 0)),
                   pl.BlockSpec((None, 1, w), lambda b, t: (b, 0, 0))],
        out_shape=[jax.ShapeDtypeStruct((nb * lb, w), F32),
                   jax.ShapeDtypeStruct((nb, 1, w), F32)],
        scratch_shapes=[pltpu.VMEM((tl + SUBLANES, w), F32), pltpu.VMEM((1, w), F32)],
        compiler_params=_cparams(("parallel", "arbitrary"), 40),
        name="rglru",
    )(rx, grg, conv_state8, h0, conv_w, conv_b, wbd, bax, lam)
    return yrec, hlast


def _stream_queries(q):
    lane = lax.broadcasted_iota(jnp.int32, q.shape, 1)
    zero = jnp.zeros_like(q)
    return jnp.concatenate([jnp.where(lane < DA_DK, q, zero), jnp.where(lane >= DA_DK, q, zero)], axis=0)


def _attn_init(m_s, l_s, acc_s):
    m_s[...] = jnp.full(m_s.shape, NEG_BIG, F32)
    l_s[...] = jnp.zeros(l_s.shape, F32)
    acc_s[...] = jnp.zeros(acc_s.shape, F32)


def _attn_update(s_t, v_t, m_s, l_s, acc_s):
    m_old = m_s[...]
    m_new = jnp.maximum(m_old, jnp.max(s_t, axis=0, keepdims=True))
    alpha = jnp.exp(m_old - m_new)
    p = jnp.exp(s_t - m_new)
    l_s[...] = alpha * l_s[...] + jnp.sum(p, axis=0, keepdims=True)
    acc_s[...] = alpha * acc_s[...] + _dot(v_t, p.astype(BF16))
    m_s[...] = m_new


def _chunk_mask(tk, tq):
    kchunk = lax.broadcasted_iota(jnp.int32, (tk, 2 * tq), 0) // CHUNK
    col = lax.broadcasted_iota(jnp.int32, (tk, 2 * tq), 1)
    qchunk = jnp.where(col >= tq, col - tq, col) // CHUNK
    return kchunk <= qchunk


def _attn_finish(lam_refs, gsub, l_s, acc_s, tq):
    lq1, lk1, lq2, lk2 = lam_refs
    lam = (jnp.exp(jnp.sum(lq1[...] * lk1[...], axis=1, keepdims=True))
           - jnp.exp(jnp.sum(lq2[...] * lk2[...], axis=1, keepdims=True)) + LAM_INIT)
    o = acc_s[...] / l_s[...]
    if tq % LANES == 0:
        o_t = o[:, :tq] - lam * o[:, tq:]
    else:
        o_t = o - lam * pltpu.roll(o, tq, 1)
    o_t = o_t * lax.rsqrt(jnp.mean(o_t * o_t, axis=0, keepdims=True) + SUBLN_EPS)
    return o_t.T[:tq, :] * gsub * (1.0 - LAM_INIT)


def _attn_prompt_kernel(lq1, lk1, lq2, lk2, gsub_ref, q_ref, k_ref, v_ref, o_ref,
                        kb_s, vt_s, m_s, l_s, acc_s, *, tq, nblk):
    qi = pl.program_id(2)

    @pl.when(qi == 0)
    def _():
        for j in range(nblk):
            rows = slice(j * tq, (j + 1) * tq)
            kb_s[rows, :] = k_ref[rows, :].astype(BF16)
            vt_s[:, rows] = v_ref[rows, :].T.astype(BF16)

    qcat = _stream_queries(q_ref[...])
    _attn_init(m_s, l_s, acc_s)

    def body(j, carry):
        start = pl.multiple_of(j * tq, tq)
        s_t = _dot_nt(kb_s[pl.ds(start, tq), :], qcat)
        _attn_update(s_t, vt_s[:, pl.ds(start, tq)], m_s, l_s, acc_s)
        return carry
    lax.fori_loop(0, qi, body, 0)

    start = pl.multiple_of(qi * tq, tq)
    s_t = jnp.where(_chunk_mask(tq, tq), _dot_nt(kb_s[pl.ds(start, tq), :], qcat), NEG_BIG)
    _attn_update(s_t, vt_s[:, pl.ds(start, tq)], m_s, l_s, acc_s)
    o_ref[...] = _attn_finish((lq1, lk1, lq2, lk2), gsub_ref[...], l_s, acc_s, tq)


def _attn_prompt_call(q, k, v, lam_params, g_subln, nb, lb, tq):
    nq = lb // tq
    vec = lambda b, h, i: (0, 0)
    in_specs = [pl.BlockSpec((1, DA_DK), vec)] * 4 + [
        pl.BlockSpec((1, DA_DV), vec),
        pl.BlockSpec((tq, DA_DV), lambda b, h, i: (b * nq + i, h)),
        pl.BlockSpec((None, lb, DA_DV), lambda b, h, i: (b, 0, h)),
        pl.BlockSpec((None, lb, DA_DV), lambda b, h, i: (b, 0, h))]
    return pl.pallas_call(
        functools.partial(_attn_prompt_kernel, tq=tq, nblk=nq),
        grid=(nb, DA_HEADS, nq),
        in_specs=in_specs,
        out_specs=pl.BlockSpec((tq, DA_DV), lambda b, h, i: (b * nq + i, h)),
        out_shape=jax.ShapeDtypeStruct((nb * lb, D_MODEL), F32),
        scratch_shapes=[pltpu.VMEM((lb, DA_DV), BF16), pltpu.VMEM((DA_DV, lb), BF16),
                        pltpu.VMEM((1, 2 * tq), F32), pltpu.VMEM((1, 2 * tq), F32),
                        pltpu.VMEM((DA_DV, 2 * tq), F32)],
        compiler_params=_cparams(("parallel", "parallel", "arbitrary"), 40),
        name="diff_attn_prompt",
    )(*lam_params, g_subln, q, k, v)


def _attn_sample_kernel(lq1, lk1, lq2, lk2, gsub_ref, q_ref, k_ref, v_ref, kp_ref, vp_ref,
                        o_ref, m_s, l_s, acc_s, *, dl, tkp):
    j = pl.program_id(1)

    @pl.when(j == 0)
    def _():
        _attn_init(m_s, l_s, acc_s)

    for h in range(DA_HEADS):
        cols = slice(h * DA_DV, (h + 1) * DA_DV)
        qcat = _stream_queries(q_ref[:, cols])
        s_t = _dot_nt(kp_ref[:, cols].astype(BF16), qcat)
        v_t = vp_ref[pl.ds(h, tkp, stride=DA_HEADS), :].T.astype(BF16)
        _attn_update(s_t, v_t, m_s.at[h], l_s.at[h], acc_s.at[h])

    @pl.when(j == pl.num_programs(1) - 1)
    def _():
        mask = _chunk_mask(dl, dl)
        for h in range(DA_HEADS):
            cols = slice(h * DA_DV, (h + 1) * DA_DV)
            qcat = _stream_queries(q_ref[:, cols])
            s_t = jnp.where(mask, _dot_nt(k_ref[:, cols].astype(BF16), qcat), NEG_BIG)
            _attn_update(s_t, v_ref[:, cols].T.astype(BF16), m_s.at[h], l_s.at[h], acc_s.at[h])
            o_ref[:, cols] = _attn_finish((lq1, lk1, lq2, lk2), gsub_ref[...], l_s.at[h], acc_s.at[h], dl)


def _attn_sample_call(q, k, v, k_past, v_past, lam_params, g_subln, nb, dl, tkp):
    p = k_past.shape[1]
    vec = lambda b, j: (0, 0)
    new = pl.BlockSpec((dl, D_MODEL), lambda b, j: (b, 0))
    keys = pl.BlockSpec((None, tkp, D_MODEL), lambda b, j: (b, j, 0))
    vals = pl.BlockSpec((None, tkp * DA_HEADS, DA_DV), lambda b, j: (b, j, 0))
    return pl.pallas_call(
        functools.partial(_attn_sample_kernel, dl=dl, tkp=tkp),
        grid=(nb, p // tkp),
        in_specs=[pl.BlockSpec((1, DA_DK), vec)] * 4 + [pl.BlockSpec((1, DA_DV), vec),
                                                         new, new, new, keys, vals],
        out_specs=new,
        out_shape=jax.ShapeDtypeStruct((nb * dl, D_MODEL), F32),
        scratch_shapes=[pltpu.VMEM((DA_HEADS, 1, 2 * dl), F32), pltpu.VMEM((DA_HEADS, 1, 2 * dl), F32),
                        pltpu.VMEM((DA_HEADS, DA_DV, 2 * dl), F32)],
        compiler_params=_cparams(("parallel", "arbitrary"), 40),
        name="diff_attn_sample",
    )(*lam_params, g_subln, q, k, v, k_past, v_past)


def _memkv_kernel(m_ref, g_ref, wk_ref, wv_ref, k_ref, v_ref):
    mn = _rmsnorm(m_ref[...], g_ref[...]).astype(BF16)
    k_ref[...] = _dot(mn, wk_ref[...])
    v_ref[...] = _dot(mn, wv_ref[...])


def _memkv_call(mem, g_mem, w_ck, w_cv, tm):
    t = mem.shape[0]
    row = lambda i: (i, 0)
    fixed = lambda i: (0, 0)
    return pl.pallas_call(
        _memkv_kernel,
        grid=(t // tm,),
        in_specs=[pl.BlockSpec((tm, D_MODEL), row), pl.BlockSpec((1, D_MODEL), fixed),
                  pl.BlockSpec((D_MODEL, D_MODEL), fixed), pl.BlockSpec((D_MODEL, D_MODEL), fixed)],
        out_specs=[pl.BlockSpec((tm, D_MODEL), row)] * 2,
        out_shape=[jax.ShapeDtypeStruct((t, D_MODEL), F32)] * 2,
        compiler_params=_cparams(("parallel",), 32),
        name="memory_kv",
    )(mem, g_mem, w_ck, w_cv)


def _mid_kernel(x_ref, ya_ref, yr_ref, sga_ref, sgr_ref, wo_ref, gc_ref, wcq_ref, mk_ref, mv_ref,
                wco_ref, gf_ref, wpq_ref, x2_ref, pq_ref, *, groups, rows):
    merged = (sga_ref[...] * ya_ref[...] + sgr_ref[...] * yr_ref[...]).astype(BF16)
    x1 = x_ref[...] + _dot(merged, wo_ref[...])
    qc = _dot(_rmsnorm(x1, gc_ref[...]).astype(BF16), wcq_ref[...]).astype(BF16)
    o_rows = []
    for g in range(groups):
        mk = mk_ref[g].astype(BF16)
        mv = mv_ref[g].astype(BF16)
        o_heads = []
        for h in range(CA_HEADS):
            cols = slice(h * CA_DH, (h + 1) * CA_DH)
            s = _dot_nt(qc[g * rows:(g + 1) * rows, cols], mk[:, cols]) * (CA_DH ** -0.5)
            p = jnp.exp(s - jnp.max(s, axis=1, keepdims=True))
            p = p / jnp.sum(p, axis=1, keepdims=True)
            o_heads.append(_dot(p.astype(BF16), mv[:, cols]))
        o_rows.append(jnp.concatenate(o_heads, axis=1))
    o = o_rows[0] if groups == 1 else jnp.concatenate(o_rows, axis=0)
    x2 = x1 + _dot(o.astype(BF16), wco_ref[...])
    x2_ref[...] = x2
    pq_ref[...] = _dot(_rmsnorm(x2, gf_ref[...]).astype(BF16), wpq_ref[...]).astype(BF16)


def _mid_call(x, y_att, y_rec, sga, sgr, w_out, g_cross, w_cq, mem_k, mem_v, w_co, g_ffn, w_pq,
              lb, tm):
    t = x.shape[0]
    rows = min(tm, lb)
    groups = tm // rows
    per_mem = lb // rows
    row = lambda i: (i, 0)
    fixed = lambda i: (0, 0)
    mem_spec = pl.BlockSpec((groups, MEM_LEN, D_MODEL), lambda i: (i // per_mem, 0, 0))
    act = pl.BlockSpec((tm, D_MODEL), row)
    wsq = pl.BlockSpec((D_MODEL, D_MODEL), fixed)
    gvec = pl.BlockSpec((1, D_MODEL), fixed)
    return pl.pallas_call(
        functools.partial(_mid_kernel, groups=groups, rows=rows),
        grid=(t // tm,),
        in_specs=[act, act, act, act, act, wsq, gvec, wsq, mem_spec, mem_spec, wsq, gvec,
                  pl.BlockSpec((D_MODEL, PK_HEADS * PK_DQ), fixed)],
        out_specs=[act, pl.BlockSpec((tm, PK_HEADS * PK_DQ), row)],
        out_shape=[jax.ShapeDtypeStruct((t, D_MODEL), F32),
                   jax.ShapeDtypeStruct((t, PK_HEADS * PK_DQ), BF16)],
        compiler_params=_cparams(("parallel",), 56),
        name="merge_cross_attn",
    )(x, y_att, y_rec, sga, sgr, w_out, g_cross, w_cq, mem_k, mem_v, w_co, g_ffn, w_pq)


def _extract_top(fill, s_ref, order, t_ref, rank_ref):
    def rounds(break_ties):
        fill()
        rank_ref[...] = jnp.full(rank_ref.shape, float(PK_TOPK), F32)

        def body(r, carry):
            s = s_ref[...]
            m = jnp.max(s, axis=0, keepdims=True)
            if break_ties:
                first = jnp.min(jnp.where(s == m, order, 1e9), axis=0, keepdims=True)
                sel = order == first
            else:
                sel = s == m
            t_ref[pl.ds(r, 1), :] = m
            s_ref[...] = jnp.where(sel, -jnp.inf, s)
            rank_ref[...] = jnp.where(sel, jnp.asarray(r, F32), rank_ref[...])
            return carry
        lax.fori_loop(0, PK_TOPK, body, 0)

    rounds(False)
    taken = jnp.sum(jnp.where(rank_ref[...] < PK_TOPK, 1.0, 0.0), axis=0, keepdims=True)
    clean = jnp.min(jnp.where(taken == PK_TOPK, 1.0, 0.0))

    @pl.when(clean < 0.5)
    def _():
        rounds(True)


def _route_head(pq_ref, k1_ref, k2_ref, h, n1_s, p_s, r2_s, q_s, zinv_s,
                s_s, rank1_s, rank2_s, cand_s, rankc_s, t1_s, t2_s, ts_s, na_s):
    tb = pq_ref.shape[0]
    key_id = lax.broadcasted_iota(jnp.int32, (PK_NKEYS, tb), 0).astype(F32)
    s1 = _dot_nt(k1_ref[h], pq_ref[:, (2 * h) * PK_DHALF:(2 * h + 1) * PK_DHALF])
    groups = [slice(c * LANES, (c + 1) * LANES) for c in range(tb // LANES)]
    p1 = jnp.exp(s1 - jnp.max(s1, axis=0, keepdims=True))
    for c, lanes in enumerate(groups):
        p_s[h, c] = p1[:, lanes]

    def fill1():
        s_s[...] = _dot_nt(k1_ref[h], pq_ref[:, (2 * h) * PK_DHALF:(2 * h + 1) * PK_DHALF])
    _extract_top(fill1, s_s, key_id, t1_s, rank1_s)
    s2 = _dot_nt(k2_ref[h], pq_ref[:, (2 * h + 1) * PK_DHALF:(2 * h + 2) * PK_DHALF])
    q2 = jnp.exp(s2 - jnp.max(s2, axis=0, keepdims=True))
    for c, lanes in enumerate(groups):
        q_s[h, c] = q2[:, lanes]

    def fill2():
        s_s[...] = _dot_nt(k2_ref[h], pq_ref[:, (2 * h + 1) * PK_DHALF:(2 * h + 2) * PK_DHALF])
    _extract_top(fill2, s_s, key_id, t2_s, rank2_s)
    for c, lanes in enumerate(groups):
        r2_s[h, c] = rank2_s[:, lanes]

    row16 = lax.broadcasted_iota(jnp.int32, (PK_TOPK, tb), 0).astype(F32)
    row8 = lax.broadcasted_iota(jnp.int32, (SUBLANES, tb), 0).astype(F32)
    last = PK_TOPK + (SUBLANES - 1) * SUBLANES
    order = [row16] + [row8 + float(a * PK_TOPK) for a in range(1, SUBLANES)]
    order.append((row8 + float(SUBLANES)) * float(PK_TOPK))

    def fill_pairs():
        t1 = t1_s[...]
        t2 = t2_s[...]
        cand_s[0:PK_TOPK, :] = t1[0:1] + t2
        for a in range(1, SUBLANES):
            lo = PK_TOPK + (a - 1) * SUBLANES
            cand_s[lo:lo + SUBLANES, :] = jnp.where(row8 < PAIR_COLS[a], t1[a:a + 1] + t2[:SUBLANES],
                                                    -jnp.inf)
        cand_s[last:, :] = t1[SUBLANES:] + t2[0:1]
    _extract_top(fill_pairs, cand_s, jnp.concatenate(order, axis=0), ts_s, rankc_s)
    picked = jnp.where(rankc_s[...] < PK_TOPK, 1.0, 0.0)
    na_s[0:1, :] = jnp.sum(picked[0:PK_TOPK], axis=0, keepdims=True)
    for a in range(1, SUBLANES):
        lo = PK_TOPK + (a - 1) * SUBLANES
        na_s[a:a + 1, :] = jnp.sum(picked[lo:lo + SUBLANES], axis=0, keepdims=True)
    na_s[SUBLANES:, :] = picked[last:]
    ts = ts_s[...]
    zinv_s[h] = 1.0 / jnp.sum(jnp.exp(ts - ts[0:1]), axis=0, keepdims=True)

    for c, lanes in enumerate(groups):
        rank1 = rank1_s[:, lanes]
        n1 = jnp.zeros((PK_NKEYS, LANES), F32)
        for a in range(PK_TOPK):
            n1 = jnp.where(rank1 == float(a), na_s[a:a + 1, lanes], n1)
        n1_s[h, c] = n1


def _peer_kernel(x2_ref, pq_ref, gf_ref, gfin_ref, k1_ref, k2_ref, u_ref, vt_ref, y_ref,
                 xft_s, n1_s, p_s, r2_s, q_s, zinv_s, s_s, rank1_s, rank2_s, cand_s, rankc_s,
                 t1_s, t2_s, ts_s, na_s, act0_s, act1_s, wa0_s, wa1_s, acc_s, *, ni):
    s = pl.program_id(1)
    n_blocks = pl.num_programs(1) - 2
    tb = xft_s.shape[1]
    eb = ni * PK_NKEYS

    @pl.when(s == 0)
    def _():
        xft_s[...] = _rmsnorm(x2_ref[...], gf_ref[...]).T.astype(BF16)
        acc_s[...] = jnp.zeros(acc_s.shape, F32)
        act1_s[...] = jnp.zeros(act1_s.shape, F32)
        wa0_s[...] = jnp.zeros(wa0_s.shape, BF16)
        for h in range(PK_HEADS):
            _route_head(pq_ref, k1_ref, k2_ref, h, n1_s, p_s, r2_s, q_s, zinv_s,
                        s_s, rank1_s, rank2_s, cand_s, rankc_s, t1_s, t2_s, ts_s, na_s)

    def stages(act_new, act_cur, wa_new, wa_done):
        blk = s - 1
        live = jnp.where((blk >= 0) & (blk < n_blocks), 1.0, 0.0)
        base = pl.multiple_of(jnp.clip(blk, 0, n_blocks - 1) * ni, ni)
        groups = [slice(c * LANES, (c + 1) * LANES) for c in range(tb // LANES)]
        n_chunks = ni // 2
        for j in range(n_chunks):
            er = slice(j * (eb // n_chunks), (j + 1) * (eb // n_chunks))
            dr = slice(j * (D_MODEL // n_chunks), (j + 1) * (D_MODEL // n_chunks))
            act_new[er, :] = _dot(u_ref[er, :], xft_s[...])
            acc_s[dr, :] += _dot(vt_ref[dr, :], wa_done[...])
            for ii in range(2 * j, 2 * j + 2):
                rows = slice(ii * PK_NKEYS, (ii + 1) * PK_NKEYS)
                for c, lanes in enumerate(groups):
                    w = jnp.zeros((PK_NKEYS, LANES), F32)
                    for h in range(PK_HEADS):
                        gate_i = p_s[h, c, pl.ds(base + ii, 1), :] * (zinv_s[h, :, lanes] * live)
                        picked = r2_s[h, c] < n1_s[h, c, pl.ds(base + ii, 1), :]
                        w = w + jnp.where(picked, q_s[h, c], 0.0) * gate_i
                    wa_new[rows, lanes] = (w * _gelu(act_cur[rows, lanes])).astype(BF16)

    @pl.when(s % 2 == 0)
    def _():
        stages(act0_s, act1_s, wa1_s, wa0_s)

    @pl.when(s % 2 == 1)
    def _():
        stages(act1_s, act0_s, wa0_s, wa1_s)

    @pl.when(s == pl.num_programs(1) - 1)
    def _():
        y_ref[...] = _rmsnorm(x2_ref[...] + acc_s[...].T, gfin_ref[...])


def _peer_call(x2, pq, g_ffn, g_final, keys1, keys2, u_tab, v_blocks, tb):
    ni = v_blocks.shape[2] // PK_NKEYS
    t = x2.shape[0]
    eb = ni * PK_NKEYS
    ne = PK_EXPERTS // eb
    n_cand = PK_TOPK + SUBLANES * SUBLANES
    fixed = lambda i, e: (0, 0)
    keys_spec = pl.BlockSpec((PK_HEADS, PK_NKEYS, PK_DHALF), lambda i, e: (0, 0, 0))
    route = pltpu.VMEM((PK_HEADS, tb // LANES, PK_NKEYS, LANES), F32)
    keys = pltpu.VMEM((PK_NKEYS, tb), F32)
    cand = pltpu.VMEM((n_cand, tb), F32)
    top = pltpu.VMEM((PK_TOPK, tb), F32)
    return pl.pallas_call(
        functools.partial(_peer_kernel, ni=ni),
        grid=(t // tb, ne + 2),
        in_specs=[pl.BlockSpec((tb, D_MODEL), lambda i, e: (i, 0), pipeline_mode=pl.Buffered(1)),
                  pl.BlockSpec((tb, PK_HEADS * PK_DQ), lambda i, e: (i, 0), pipeline_mode=pl.Buffered(1)),
                  pl.BlockSpec((1, D_MODEL), fixed), pl.BlockSpec((1, D_MODEL), fixed),
                  keys_spec, keys_spec,
                  pl.BlockSpec((eb, D_MODEL), lambda i, e: (jnp.minimum(e, ne - 1), 0)),
                  pl.BlockSpec((None, D_MODEL, eb), lambda i, e: (jnp.clip(e - 2, 0, ne - 1), 0, 0))],
        out_specs=pl.BlockSpec((tb, D_MODEL), lambda i, e: (i, 0)),
        out_shape=jax.ShapeDtypeStruct((t, D_MODEL), F32),
        scratch_shapes=[pltpu.VMEM((D_MODEL, tb), BF16), route, route, route, route,
                        pltpu.VMEM((PK_HEADS, 1, tb), F32), keys, keys, keys, cand, cand,
                        top, top, top, top,
                        pltpu.VMEM((eb, tb), F32), pltpu.VMEM((eb, tb), F32),
                        pltpu.VMEM((eb, tb), BF16), pltpu.VMEM((eb, tb), BF16),
                        pltpu.VMEM((D_MODEL, tb), F32)],
        compiler_params=_cparams(("parallel", "arbitrary"), 56),
        name="peer",
    )(x2, pq, g_ffn, g_final, keys1, keys2, u_tab, v_blocks)


def _rope_tables(start, n_pos, rows):
    half = DA_DK // 2
    pos = start + jnp.arange(n_pos, dtype=jnp.int32)
    inv = jnp.exp(-math.log(ROPE_THETA) * jnp.arange(half, dtype=F32) * (2.0 / DA_DK))
    ang = pos.astype(F32)[:, None] * inv[None, :]
    cos = jnp.tile(jnp.cos(ang), (1, LANES // half))
    sin = jnp.tile(jnp.concatenate([-jnp.sin(ang), jnp.sin(ang)], axis=1), (1, LANES // DA_DK))
    reps = max(1, rows // n_pos)
    return jnp.tile(cos, (reps, 1)), jnp.tile(sin, (reps, 1))


def _block_diag_gates(wa, wx):
    per = RG_GROUP // RG_BW
    def bd(w):
        w = w.reshape(RG_WIDTH // RG_GROUP, per, RG_BW, RG_BW)
        eye = jnp.eye(per, dtype=w.dtype)
        return jnp.einsum("gmcd,mn->gmcnd", w, eye).reshape(RG_WIDTH // RG_GROUP, RG_GROUP, RG_GROUP)
    return jnp.concatenate([bd(wa), bd(wx)], axis=2).astype(BF16)


def _group_forward(x, pos0, past_kv, conv_state, h0, mem_kv, wts, tiles):
    nb, lb, d = x.shape
    t = nb * lb
    tm, tl, tq, tmid, tb = tiles
    xf = x.reshape(t, d)
    cos_t, sin_t = _rope_tables(pos0, lb, tm)
    q, k, v, rx, grg, sga, sgr = _inproj_call(xf, wts["g_mix"], wts["w_in"], cos_t, sin_t, tm)

    conv8 = jnp.concatenate(
        [jnp.zeros((nb, SUBLANES - (CONV_W - 1), RG_WIDTH), F32), conv_state.astype(F32)], axis=1)
    y_rec, h_last = _rglru_call(rx, grg, conv8, h0.reshape(nb, 1, RG_WIDTH), wts["conv_w"],
                                wts["conv_b"], wts["wbd"], wts["bax"], wts["rg_lambda"], nb, lb, tl)
    new_conv = jnp.concatenate([conv_state.astype(F32), rx.reshape(nb, lb, RG_WIDTH)],
                               axis=1)[:, lb:]

    if past_kv is None:
        y_att = _attn_prompt_call(q, k.reshape(nb, lb, d), v.reshape(nb, lb, d), wts["lam"],
                                  wts["g_subln"], nb, lb, tq)
    else:
        y_att = _attn_sample_call(q, k, v, past_kv[0], past_kv[1], wts["lam"], wts["g_subln"], nb, lb,
                                  _pick(past_kv[0].shape[1], 1024))

    x2, pq = _mid_call(xf, y_att, y_rec, sga, sgr, wts["w_out"], wts["g_cross"], wts["w_cq"],
                       mem_kv[0], mem_kv[1], wts["w_co"], wts["g_ffn"], wts["w_pq"], lb, tmid)
    y = _peer_call(x2, pq, wts["g_ffn"], wts["g_final"], wts["keys1"], wts["keys2"],
                   wts["u_tab"], wts["v_blocks"], tb)
    return (y.reshape(nb, lb, d), k.reshape(nb, lb, 2 * DA_HEADS, DA_DK),
            v.reshape(nb, lb, DA_HEADS, DA_DV), new_conv, h_last.reshape(nb, RG_WIDTH))


def _pick(n, pref):
    t = min(n, pref)
    while n % t:
        t //= 2
    return t


def kernel(x_prompt, x_sample, cache_attn_k, cache_attn_v, cache_mem_k, cache_mem_v, state_conv, state_rglru, mem_prompt, g_mix, w_in, lam_q1, lam_k1, lam_q2, lam_k2, g_subln, conv_w, conv_b, rg_wa, rg_ba, rg_wx, rg_bx, rg_lambda, w_out, g_cross, g_mem, w_cq, w_ck, w_cv, w_co, g_ffn, w_pq, pk_keys1, pk_keys2, peer_u, peer_v, g_final):
    assert g_mix.shape[0] == 1, "single-layer configuration"
    b, l, d = x_prompt.shape
    db, dl, _ = x_sample.shape
    past = cache_attn_k.shape[2]
    assert past % CHUNK == 0 and dl <= CHUNK and l % CHUNK == 0

    row = lambda a: a[0].reshape(1, -1).astype(F32)
    wts = dict(
        g_mix=row(g_mix), w_in=w_in[0].astype(BF16),
        lam=[row(lam_q1), row(lam_k1), row(lam_q2), row(lam_k2)], g_subln=row(g_subln),
        conv_w=conv_w[0], conv_b=row(conv_b),
        wbd=_block_diag_gates(rg_wa[0], rg_wx[0]),
        bax=jnp.stack([rg_ba[0], rg_bx[0]]), rg_lambda=row(rg_lambda),
        w_out=w_out[0].astype(BF16), g_cross=row(g_cross), w_cq=w_cq[0].astype(BF16),
        w_co=w_co[0].astype(BF16), g_ffn=row(g_ffn), w_pq=w_pq[0].astype(BF16),
        keys1=pk_keys1[0].astype(BF16), keys2=pk_keys2[0].astype(BF16),
        u_tab=peer_u[0].astype(BF16),
        v_blocks=peer_v[0].astype(BF16).reshape(-1, PEER_BLOCK_EXPERTS, d).transpose(0, 2, 1),
        g_final=g_final.reshape(1, -1),
    )

    mk_p, mv_p = _memkv_call(mem_prompt.reshape(b * MEM_LEN, d), row(g_mem), w_ck[0].astype(BF16),
                             w_cv[0].astype(BF16), _pick(b * MEM_LEN, 256))
    mem_p = (mk_p.reshape(b, MEM_LEN, d), mv_p.reshape(b, MEM_LEN, d))
    tiles_p = (_pick(b * l, 256), _pick(l, 256), _pick(l, 1024), _pick(l, 256),
               _pick(b * l, 512))
    yp, kp, vp, cp, hp = _group_forward(
        x_prompt, 0, None,
        jnp.zeros((b, CONV_W - 1, RG_WIDTH), F32), jnp.zeros((b, RG_WIDTH), F32), mem_p, wts, tiles_p)

    tiles_s = (_pick(db * dl, 256), _pick(dl, 256), dl, _pick(db * dl, 256),
               _pick(db * dl, 512))
    ys, ks, vs, cs, hs = _group_forward(
        x_sample, past,
        (cache_attn_k[0].reshape(db, past, d), cache_attn_v[0].reshape(db, past * DA_HEADS, DA_DV)),
        state_conv[0], state_rglru[0],
        (cache_mem_k[0].reshape(db, MEM_LEN, d), cache_mem_v[0].reshape(db, MEM_LEN, d)), wts, tiles_s)

    return (yp, ys, kp[None], vp[None],
            mk_p.reshape(1, b, MEM_LEN, CA_HEADS, CA_DH), mv_p.reshape(1, b, MEM_LEN, CA_HEADS, CA_DH),
            cp[None], hp[None], ks[None], vs[None], cs[None], hs[None])
```
